```python
import math
import jax, jax.numpy as jnp
from jax import lax
import numpy as np

D_MODEL = 1024
BATCH = 8
SEQ = 4096
DEPTH = 4

N_MEM = 256
NORM_EPS = 1e-6
A_HEAD_DIM = 64
A_WIDTH = D_MODEL // 2
A_HEADS = A_WIDTH // A_HEAD_DIM
A_DECAY_LORA = 64
A_ICLR_LORA = 64
A_GATE_LORA = 128
A_COLS = 3 * A_WIDTH + A_DECAY_LORA + A_ICLR_LORA + A_GATE_LORA
A_GN_EPS = 64e-5
B_HEAD_DIM = 64
B_WIDTH = D_MODEL // 2
B_HEADS = B_WIDTH // B_HEAD_DIM
B_COLS = 4 * B_WIDTH
RET_CHUNK = 128
ROPE_BASE = 10000.0
HYB_COLS = A_COLS + B_COLS
C_HEAD_DIM = 64
C_V_DIM = 2 * C_HEAD_DIM
C_HEADS = D_MODEL // C_V_DIM
Q_BLOCK = 128
REL_BUCKETS = 32
REL_MAX_DIST = 128
X_HEADS = 4
X_HEAD_DIM = D_MODEL // X_HEADS
D_FF = 4 * D_MODEL
N_EVEN = (DEPTH + 1) // 2
N_ODD = DEPTH // 2

kernel_name = "hybrid_rwkv7_retnet_diffattn_trunk"


def rms_norm(x, w, eps=NORM_EPS):
    xf = x.astype(jnp.float32)
    y = xf * lax.rsqrt(jnp.mean(xf * xf, axis=-1, keepdims=True) + eps)
    return y.astype(x.dtype) * w


def group_norm(x, w, b, eps):
    xf = x.astype(jnp.float32)
    mu = jnp.mean(xf, axis=-1, keepdims=True)
    var = jnp.mean(jnp.square(xf - mu), axis=-1, keepdims=True)
    y = ((xf - mu) * lax.rsqrt(var + eps)).astype(x.dtype)
    h, d = x.shape[-2:]
    return y * w.reshape(h, d) + b.reshape(h, d)


def token_shift(x):
    return jnp.pad(x[:, :-1], ((0, 0), (1, 0), (0, 0)))


def rotary(x):
    s, d = x.shape[1], x.shape[-1]
    inv = ROPE_BASE ** (-jnp.arange(0, d, 2, dtype=jnp.float32) / d)
    ang = jnp.arange(s, dtype=jnp.float32)[:, None] * inv[None, :]
    cos = jnp.cos(ang)[None, :, None, :]
    sin = jnp.sin(ang)[None, :, None, :]
    x1, x2 = jnp.split(x, 2, axis=-1)
    return jnp.concatenate([x1 * cos - x2 * sin, x1 * sin + x2 * cos], axis=-1).astype(x.dtype)


def t5_bucket(rel):
    n = jnp.maximum(rel, 0)
    max_exact = REL_BUCKETS // 2
    large = max_exact + (jnp.log(jnp.maximum(n, max_exact).astype(jnp.float32) / max_exact)
                         / math.log(REL_MAX_DIST / max_exact) * (REL_BUCKETS - max_exact)).astype(jnp.int32)
    large = jnp.minimum(large, REL_BUCKETS - 1)
    return jnp.where(n < max_exact, n, large)


def wkv7_scan(r, w, k, v, a, b):
    dt = r.dtype
    bsz, _, h, n = r.shape
    xs = tuple(jnp.moveaxis(t.astype(jnp.float32), 1, 0) for t in (r, w, k, v, a, b))

    def step(state, inp):
        r_t, w_t, k_t, v_t, a_t, b_t = inp
        sa = jnp.einsum('bhvk,bhk->bhv', state, a_t)
        state = (state * w_t[:, :, None, :] + sa[..., None] * b_t[:, :, None, :]
                 + v_t[..., None] * k_t[:, :, None, :])
        return state, jnp.einsum('bhvk,bhk->bhv', state, r_t)

    s0 = jnp.zeros((bsz, h, n, n), jnp.float32)
    _, ys = lax.scan(step, s0, xs)
    return jnp.moveaxis(ys, 0, 1).astype(dt)


def rwkv7_time_mix(pa, mu, w0, w2, a0, a2, g2, k_k, k_a, r_k, ln_w, ln_b):
    bsz, s, _ = pa.shape
    pa = pa + (token_shift(pa) - pa) * mu
    c1 = A_WIDTH
    c3 = 3 * A_WIDTH
    r, k, v, wd, ad, gd = jnp.split(pa, [c1, 2 * c1, c3, c3 + A_DECAY_LORA, c3 + A_DECAY_LORA + A_ICLR_LORA], axis=-1)
    w_log = -jax.nn.softplus(-(w0 + jnp.tanh(wd) @ w2)) - 0.5
    decay = jnp.exp(-jnp.exp(w_log.astype(jnp.float32)))
    a = jax.nn.sigmoid(a0 + ad @ a2)
    g = jax.nn.sigmoid(gd) @ g2
    hd = lambda t: t.reshape(bsz, s, A_HEADS, A_HEAD_DIM)
    kk = hd(k * k_k).astype(jnp.float32)
    kk = (kk / jnp.maximum(jnp.sqrt(jnp.sum(kk * kk, axis=-1, keepdims=True)), 1e-12)).astype(pa.dtype)
    k = k * (1.0 + (a - 1.0) * k_a)
    r_h, k_h, v_h, a_h = hd(r), hd(k), hd(v), hd(a)
    y = wkv7_scan(r_h, hd(decay), k_h, v_h, -kk, kk * a_h)
    y = group_norm(y, ln_w, ln_b, A_GN_EPS)
    y = y + jnp.sum(r_h * k_h * r_k, axis=-1, keepdims=True) * v_h
    return y.reshape(bsz, s, A_WIDTH) * g


def retention(pb, ln_w):
    bsz, s, _ = pb.shape
    q, k, v, g = jnp.split(pb, 4, axis=-1)
    q = rotary(q.reshape(bsz, s, B_HEADS, B_HEAD_DIM))
    k = rotary(k.reshape(bsz, s, B_HEADS, B_HEAD_DIM)) * (B_HEAD_DIM ** -0.5)
    v = v.reshape(bsz, s, B_HEADS, B_HEAD_DIM)
    n_chunks = s // RET_CHUNK
    to_chunks = lambda t: t.reshape(bsz, n_chunks, RET_CHUNK, B_HEADS, -1).transpose(0, 3, 1, 2, 4).astype(jnp.float32)
    qc, kc, vc = to_chunks(q), to_chunks(k), to_chunks(v)
    gamma = 1.0 - 2.0 ** (-5.0 - jnp.arange(B_HEADS, dtype=jnp.float32))
    lg = jnp.log(gamma)[:, None]
    idx = jnp.arange(RET_CHUNK, dtype=jnp.float32)
    rel = idx[:, None] - idx[None, :]
    dmat = jnp.where(rel >= 0, jnp.exp(jnp.maximum(rel, 0.0)[None] * lg[..., None]), 0.0)
    scores = jnp.einsum('bhncd,bhnmd->bhncm', qc, kc) * dmat[None, :, None]
    inner = jnp.einsum('bhncm,bhnme->bhnce', scores, vc)
    zeta = jnp.exp((RET_CHUNK - 1 - idx)[None, :] * lg)
    xi = jnp.exp((idx + 1)[None, :] * lg)
    chunk_decay = jnp.exp(RET_CHUNK * lg)[:, :, None]
    upd = jnp.einsum('bhnmd,bhnme->bhnde', kc * zeta[None, :, None, :, None], vc)

    def step(state, u_i):
        return state * chunk_decay[None] + u_i, state

    s0 = jnp.zeros((bsz, B_HEADS, B_HEAD_DIM, B_HEAD_DIM), jnp.float32)
    _, r_prev = lax.scan(step, s0, jnp.moveaxis(upd, 2, 0))
    r_prev = jnp.moveaxis(r_prev, 0, 2)
    cross = jnp.einsum('bhncd,bhnde->bhnce', qc * xi[None, :, None, :, None], r_prev)
    o = (inner + cross).transpose(0, 2, 3, 1, 4).reshape(bsz, s, B_HEADS, B_HEAD_DIM).astype(pb.dtype)
    o = rms_norm(o, ln_w.reshape(B_HEADS, B_HEAD_DIM))
    return o.reshape(bsz, s, B_WIDTH) * jax.nn.silu(g)


def diff_attention(h, w_in, lq1, lk1, lq2, lk2, ln_w, w_out, rel_bias, lam_init):
    bsz, s, _ = h.shape
    q, k, v = jnp.split(h @ w_in, 3, axis=-1)
    q = q.reshape(bsz, s, C_HEADS, 2, C_HEAD_DIM).transpose(0, 2, 3, 1, 4)
    k = k.reshape(bsz, s, C_HEADS, 2, C_HEAD_DIM).transpose(0, 2, 3, 1, 4)
    v = v.reshape(bsz, s, C_HEADS, C_V_DIM).transpose(0, 2, 1, 3)
    lam = (jnp.exp(jnp.sum(lq1 * lk1).astype(jnp.float32))
           - jnp.exp(jnp.sum(lq2 * lk2).astype(jnp.float32)) + lam_init)
    n_blk = s // Q_BLOCK
    qb = q.reshape(bsz, C_HEADS, 2, n_blk, Q_BLOCK, C_HEAD_DIM).transpose(3, 0, 1, 2, 4, 5)
    k_pos = jnp.arange(s, dtype=jnp.int32)
    scale = C_HEAD_DIM ** -0.5

    def block(args):
        q_blk, blk = args
        q_pos = blk * Q_BLOCK + jnp.arange(Q_BLOCK, dtype=jnp.int32)
        rel = q_pos[:, None] - k_pos[None, :]
        bias = rel_bias[t5_bucket(rel)].transpose(2, 0, 1).astype(jnp.float32)
        logits = jnp.einsum('bhcqd,bhckd->bhcqk', q_blk, k).astype(jnp.float32) * scale + bias[None, :, None]
        logits = jnp.where(rel >= 0, logits, -jnp.inf)
        p = jax.nn.softmax(logits, axis=-1)
        attn = p[:, :, 0] - lam * p[:, :, 1]
        return jnp.einsum('bhqk,bhke->bhqe', attn.astype(v.dtype), v)

    o = lax.map(block, (qb, jnp.arange(n_blk, dtype=jnp.int32)))
    o = o.transpose(1, 0, 3, 2, 4).reshape(bsz, s, C_HEADS, C_V_DIM)
    o = rms_norm(o, ln_w) * (1.0 - lam_init)
    return o.reshape(bsz, s, C_HEADS * C_V_DIM) @ w_out


def memory_cross_attention(h, mem_n, w_q, w_kv, w_o):
    bsz, s, _ = h.shape
    m = mem_n.shape[1]
    q = (h @ w_q).reshape(bsz, s, X_HEADS, X_HEAD_DIM)
    k, v = jnp.split(mem_n @ w_kv, 2, axis=-1)
    k = k.reshape(bsz, m, X_HEADS, X_HEAD_DIM)
    v = v.reshape(bsz, m, X_HEADS, X_HEAD_DIM)
    logits = jnp.einsum('bshd,bmhd->bhsm', q, k).astype(jnp.float32) * (X_HEAD_DIM ** -0.5)
    p = jax.nn.softmax(logits, axis=-1)
    o = jnp.einsum('bhsm,bmhd->bshd', p.astype(v.dtype), v)
    return o.reshape(bsz, s, D_MODEL) @ w_o


def squared_relu_mlp(h, w1, w2):
    return jnp.square(jax.nn.relu(h @ w1)) @ w2


def setup_inputs(seed: int = 0) -> dict:
    key = jax.random.key(seed)
    ks = iter(jax.random.split(key, 48))
    f32 = jnp.float32
    D = D_MODEL

    def nrm(shape, scale):
        return jax.random.normal(next(ks), shape, f32) * scale

    def gain(shape):
        return 1.0 + nrm(shape, 0.02)

    return {
        "x": nrm((BATCH, SEQ, D), 1.0),
        "mem": nrm((BATCH, N_MEM, D), 1.0),
        "rel_bias": nrm((REL_BUCKETS, C_HEADS), 0.5),
        "mem_norm_w": gain((D,)),
        "final_norm_w": gain((D,)),
        "norm_mix_w": gain((DEPTH, D)),
        "norm_cross_w": gain((DEPTH, D)),
        "norm_mlp_w": gain((DEPTH, D)),
        "xattn_w_q": nrm((DEPTH, D, D), D ** -0.5),
        "xattn_w_kv": nrm((DEPTH, D, 2 * D), D ** -0.5),
        "xattn_w_o": nrm((DEPTH, D, D), D ** -0.5),
        "mlp_w1": nrm((DEPTH, D, D_FF), D ** -0.5),
        "mlp_w2": nrm((DEPTH, D_FF, D), D_FF ** -0.5),
        "hyb_w_in": nrm((N_EVEN, D, HYB_COLS), D ** -0.5),
        "rwkv_mu": jax.random.uniform(next(ks), (N_EVEN, A_COLS), f32),
        "rwkv_w0": jax.random.uniform(next(ks), (N_EVEN, A_WIDTH), f32, minval=-6.5, maxval=-1.0),
        "rwkv_w2": nrm((N_EVEN, A_DECAY_LORA, A_WIDTH), 0.5 * A_DECAY_LORA ** -0.5),
        "rwkv_a0": nrm((N_EVEN, A_WIDTH), 0.5),
        "rwkv_a2": nrm((N_EVEN, A_ICLR_LORA, A_WIDTH), A_ICLR_LORA ** -0.5),
        "rwkv_g2": nrm((N_EVEN, A_GATE_LORA, A_WIDTH), A_GATE_LORA ** -0.5),
        "rwkv_k_k": 0.85 + nrm((N_EVEN, A_WIDTH), 0.05),
        "rwkv_k_a": 1.0 + nrm((N_EVEN, A_WIDTH), 0.05),
        "rwkv_r_k": nrm((N_EVEN, A_HEADS, A_HEAD_DIM), 0.1),
        "rwkv_ln_w": gain((N_EVEN, A_WIDTH)),
        "rwkv_ln_b": nrm((N_EVEN, A_WIDTH), 0.02),
        "ret_ln_w": gain((N_EVEN, B_WIDTH)),
        "hyb_w_out": nrm((N_EVEN, A_WIDTH + B_WIDTH, D), (A_WIDTH + B_WIDTH) ** -0.5),
        "diff_w_in": nrm((N_ODD, D, 3 * D), D ** -0.5),
        "diff_lq1": nrm((N_ODD, C_HEAD_DIM), 0.1),
        "diff_lk1": nrm((N_ODD, C_HEAD_DIM), 0.1),
        "diff_lq2": nrm((N_ODD, C_HEAD_DIM), 0.1),
        "diff_lk2": nrm((N_ODD, C_HEAD_DIM), 0.1),
        "diff_ln_w": gain((N_ODD, C_V_DIM)),
        "diff_w_out": nrm((N_ODD, C_HEADS * C_V_DIM, D), D ** -0.5),
    }


def reference(x, mem, rel_bias, mem_norm_w, final_norm_w, norm_mix_w, norm_cross_w, norm_mlp_w,
              xattn_w_q, xattn_w_kv, xattn_w_o, mlp_w1, mlp_w2, hyb_w_in, rwkv_mu, rwkv_w0, rwkv_w2,
              rwkv_a0, rwkv_a2, rwkv_g2, rwkv_k_k, rwkv_k_a, rwkv_r_k, rwkv_ln_w, rwkv_ln_b, ret_ln_w,
              hyb_w_out, diff_w_in, diff_lq1, diff_lk1, diff_lq2, diff_lk2, diff_ln_w, diff_w_out):
    mem_n = rms_norm(mem, mem_norm_w)
    h = x
    for layer in range(DEPTH):
        i = layer // 2
        hn = rms_norm(h, norm_mix_w[layer])
        if layer % 2 == 0:
            proj = hn @ hyb_w_in[i]
            y_a = rwkv7_time_mix(proj[..., :A_COLS], rwkv_mu[i], rwkv_w0[i], rwkv_w2[i], rwkv_a0[i],
                                 rwkv_a2[i], rwkv_g2[i], rwkv_k_k[i], rwkv_k_a[i], rwkv_r_k[i],
                                 rwkv_ln_w[i], rwkv_ln_b[i])
            y_b = retention(proj[..., A_COLS:], ret_ln_w[i])
            mix = jnp.concatenate([y_a, y_b], axis=-1) @ hyb_w_out[i]
        else:
            lam_init = 0.8 - 0.6 * math.exp(-0.3 * layer)
            mix = diff_attention(hn, diff_w_in[i], diff_lq1[i], diff_lk1[i], diff_lq2[i], diff_lk2[i],
                                 diff_ln_w[i], diff_w_out[i], rel_bias, lam_init)
        h = h + mix
        h = h + memory_cross_attention(rms_norm(h, norm_cross_w[layer]), mem_n,
                                       xattn_w_q[layer], xattn_w_kv[layer], xattn_w_o[layer])
        h = h + squared_relu_mlp(rms_norm(h, norm_mlp_w[layer]), mlp_w1[layer], mlp_w2[layer])
    return rms_norm(h, final_norm_w)
```

```python
import functools
import math

import jax
import jax.numpy as jnp
from jax import lax
from jax.experimental import pallas as pl
from jax.experimental.pallas import tpu as pltpu

BF = jnp.bfloat16
F32 = jnp.float32

NORM_EPS = 1e-6
HEAD_DIM = 64
PAIR = 2 * HEAD_DIM
A_GN_EPS = 64e-5
ROPE_BASE = 10000.0
REL_BUCKETS = 32
REL_MAX_DIST = 128
X_HEADS = 4
WKV_CHUNK = 64
RET_CHUNK = 128
ATTN_TILE = 256
MASK_VALUE = -1e30
VMEM_LIMIT_BYTES = 56 * 1024 * 1024


def _dot(a, b):
    return jnp.dot(a, b, preferred_element_type=F32)


def _dot_nt(a, b):
    return lax.dot_general(a, b, (((1,), (1,)), ((), ())), preferred_element_type=F32)


def _dot_tn(a, b):
    return lax.dot_general(a, b, (((0,), (0,)), ((), ())), preferred_element_type=F32)


def _rms(x, g):
    ms = jnp.mean(x * x, axis=-1, keepdims=True)
    return x * lax.rsqrt(ms + NORM_EPS) * g


def _split_bf16(x, parts):
    out = []
    for _ in range(parts):
        hi = x.astype(BF)
        out.append(hi)
        x = x - hi.astype(F32)
    return out


def _head_sum(x, ones_blk):
    return sum(_dot(part, ones_blk) for part in _split_bf16(x, 2))


def _pair_consts(rows):
    lane = lax.broadcasted_iota(jnp.int32, (rows, PAIR), 1)
    lo_mask = lane < HEAD_DIM
    r = lax.broadcasted_iota(jnp.int32, (PAIR, PAIR), 0)
    c = lax.broadcasted_iota(jnp.int32, (PAIR, PAIR), 1)
    ones_blk = jnp.where((r < HEAD_DIM) == (c < HEAD_DIM), 1.0, 0.0).astype(BF)
    return lo_mask, ones_blk


def _stack_heads(x, lo_mask):
    return jnp.concatenate([jnp.where(lo_mask, x, 0.0), jnp.where(lo_mask, 0.0, x)], axis=0)


def _params(*sem):
    return pltpu.CompilerParams(dimension_semantics=sem, vmem_limit_bytes=VMEM_LIMIT_BYTES)


def _resident(shape):
    nd = len(shape)
    return pl.BlockSpec(shape, lambda *_: (0,) * nd, pipeline_mode=pl.Buffered(1))


def _norm_matmul_kernel(x_ref, g_ref, w_ref, *o_refs, n_chunk):
    xn = _rms(x_ref[...], g_ref[...]).astype(BF)
    col = 0
    for o_ref in o_refs:
        n = o_ref.shape[-1]
        for c in range(0, n, n_chunk):
            cc = min(n_chunk, n - c)
            o_ref[:, c:c + cc] = _dot(xn, w_ref[:, col + c:col + c + cc]).astype(o_ref.dtype)
        col += n


def norm_matmul(x, g, w, splits, dtypes, tm=512, n_chunk=512):
    m, k = x.shape
    tm = min(tm, m)
    assert m % tm == 0 and sum(splits) == w.shape[1]
    return pl.pallas_call(
        functools.partial(_norm_matmul_kernel, n_chunk=n_chunk),
        grid=(m // tm,),
        in_specs=[pl.BlockSpec((tm, k), lambda i: (i, 0)), _resident((1, k)), _resident(w.shape)],
        out_specs=[pl.BlockSpec((tm, n), lambda i: (i, 0)) for n in splits],
        out_shape=[jax.ShapeDtypeStruct((m, n), dt) for n, dt in zip(splits, dtypes)],
        compiler_params=_params("parallel"),
        name="norm_matmul",
    )(x, g.reshape(1, k), w)


def _matmul_res_kernel(*refs, n_in):
    y_refs, w_refs, res_ref, o_ref = refs[:n_in], refs[n_in:2 * n_in], refs[2 * n_in], refs[2 * n_in + 1]
    acc = res_ref[...]
    for y_ref, w_ref in zip(y_refs, w_refs):
        acc = acc + _dot(y_ref[...], w_ref[...])
    o_ref[...] = acc


def matmul_residual(ys, ws, res, tm=512):
    m, n = res.shape
    tm = min(tm, m)
    n_in = len(ys)
    return pl.pallas_call(
        functools.partial(_matmul_res_kernel, n_in=n_in),
        grid=(m // tm,),
        in_specs=([pl.BlockSpec((tm, y.shape[1]), lambda i: (i, 0)) for y in ys]
                  + [_resident(w.shape) for w in ws]
                  + [pl.BlockSpec((tm, n), lambda i: (i, 0))]),
        out_specs=pl.BlockSpec((tm, n), lambda i: (i, 0)),
        out_shape=jax.ShapeDtypeStruct((m, n), F32),
        compiler_params=_params("parallel"),
        name="matmul_residual",
    )(*ys, *ws, res)


def _mlp_kernel(h_ref, g_ref, w1_ref, w2_ref, gf_ref, o_ref, a_scr, *, f_chunk, final_norm):
    h = h_ref[...]
    xn = _rms(h, g_ref[...]).astype(BF)
    for c in range(0, a_scr.shape[1], f_chunk):
        a = jnp.maximum(_dot(xn, w1_ref[:, c:c + f_chunk]), 0.0)
        a_scr[:, c:c + f_chunk] = (a * a).astype(BF)
    out = h + _dot(a_scr[...], w2_ref[...])
    if final_norm:
        out = _rms(out, gf_ref[...])
    o_ref[...] = out


def mlp_block(h, g, w1, w2, g_final, final_norm, tm=512, f_chunk=512):
    m, d = h.shape
    tm = min(tm, m)
    f = w1.shape[1]
    return pl.pallas_call(
        functools.partial(_mlp_kernel, f_chunk=f_chunk, final_norm=final_norm),
        grid=(m // tm,),
        in_specs=[pl.BlockSpec((tm, d), lambda i: (i, 0)), _resident((1, d)), _resident(w1.shape),
                  _resident(w2.shape), _resident((1, d))],
        out_specs=pl.BlockSpec((tm, d), lambda i: (i, 0)),
        out_shape=jax.ShapeDtypeStruct((m, d), F32),
        scratch_shapes=[pltpu.VMEM((tm, f), BF)],
        compiler_params=_params("parallel"),
        name="mlp_block",
    )(h, g.reshape(1, d), w1, w2, g_final.reshape(1, d))


def _xattn_kernel(h_ref, g_ref, wq_ref, kv_ref, wo_ref, o_ref, a_scr):
    h = h_ref[0]
    d = h.shape[-1]
    hd = d // X_HEADS
    xn = _rms(h, g_ref[...]).astype(BF)
    q = (_dot(xn, wq_ref[...]) * (hd ** -0.5)).astype(BF)
    for i in range(X_HEADS):
        k = kv_ref[0, :, i * hd:(i + 1) * hd]
        v = kv_ref[0, :, d + i * hd:d + (i + 1) * hd]
        s = _dot_nt(q[:, i * hd:(i + 1) * hd], k)
        p = jnp.exp(s - jnp.max(s, axis=-1, keepdims=True))
        o = _dot(p.astype(BF), v) / jnp.sum(p, axis=-1, keepdims=True)
        a_scr[:, i * hd:(i + 1) * hd] = o.astype(BF)
    o_ref[0] = h + _dot(a_scr[...], wo_ref[...])


def xattn_block(h, g, wq, kv, wo, tm=512):
    b, s, d = h.shape
    tm = min(tm, s)
    return pl.pallas_call(
        _xattn_kernel,
        grid=(b, s // tm),
        in_specs=[pl.BlockSpec((1, tm, d), lambda i, j: (i, j, 0)), _resident((1, d)), _resident(wq.shape),
                  pl.BlockSpec((1,) + kv.shape[1:], lambda i, j: (i, 0, 0)), _resident(wo.shape)],
        out_specs=pl.BlockSpec((1, tm, d), lambda i, j: (i, j, 0)),
        out_shape=jax.ShapeDtypeStruct((b, s, d), F32),
        scratch_shapes=[pltpu.VMEM((tm, d), BF)],
        compiler_params=_params("parallel", "parallel"),
        name="xattn_block",
    )(h, g.reshape(1, d), wq, kv, wo)


def _rwkv_kernel(pa_ref, prev_ref, mu_ref, wwa_ref, g2_ref, vec_ref, o_ref, s_scr):
    ci = pl.program_id(1)
    c = pa_ref.shape[1]
    w = 4 * PAIR

    @pl.when(ci == 0)
    def _():
        s_scr[...] = jnp.zeros_like(s_scr)

    pa = pa_ref[0]
    prev_last = jnp.where(ci == 0, 0.0, prev_ref[0][7:8, :])
    row = lax.broadcasted_iota(jnp.int32, pa.shape, 0)
    shifted = jnp.where(row == 0, prev_last, pltpu.roll(pa, 1, axis=0))
    x = pa + (shifted - pa) * mu_ref[...]

    r, k, v = x[:, 0:w], x[:, w:2 * w], x[:, 2 * w:3 * w]
    wa, gd = x[:, 3 * w:3 * w + PAIR], x[:, 3 * w + PAIR:3 * w + 2 * PAIR]
    w0, a0, k_k, k_a, r_k, ln_w, ln_b = (vec_ref[i:i + 1, :] for i in range(7))

    lo_mask, ones_blk = _pair_consts(c)
    z = jnp.where(lo_mask, jnp.tanh(wa), wa).astype(BF)
    twa = _dot(z, wwa_ref[...])
    neg = -(w0 + twa[:, :w])
    softplus = jnp.maximum(neg, 0.0) + jnp.log1p(jnp.exp(-jnp.abs(neg)))
    logw = -jnp.exp(-softplus - 0.5)
    a = jax.nn.sigmoid(a0 + twa[:, w:])
    g = _dot(jax.nn.sigmoid(gd).astype(BF), g2_ref[...])

    kk = k * k_k
    kk_sq = kk * kk
    ss = jnp.concatenate([_head_sum(kk_sq[:, p * PAIR:(p + 1) * PAIR], ones_blk) for p in range(4)], axis=1)
    kk = kk / jnp.maximum(jnp.sqrt(ss), 1e-12)
    k = k * (1.0 + (a - 1.0) * k_a)
    aa = -kk
    bb = kk * a

    tr = lax.broadcasted_iota(jnp.int32, (c, c), 0)
    tc = lax.broadcasted_iota(jnp.int32, (c, c), 1)
    tril = jnp.where(tr >= tc, 1.0, 0.0).astype(BF)
    cum = sum(_dot(tril, part) for part in _split_bf16(logw, 3))

    sr = lax.broadcasted_iota(jnp.int32, (2 * c, 2 * c), 0)
    sc = lax.broadcasted_iota(jnp.int32, (2 * c, 2 * c), 1)
    blk_xor = sr ^ sc
    strict, incl = sr % c > sc % c, sr % c >= sc % c

    ys = []
    for p in range(4):
        sl = slice(p * PAIR, (p + 1) * PAIR)
        cum_p, r_p, k_p, v_p, a_p, b_p = cum[:, sl], r[:, sl], k[:, sl], v[:, sl], aa[:, sl], bb[:, sl]
        e_in = jnp.exp(cum_p)
        e_out = jnp.exp(-cum_p)
        e_prev = jnp.exp(cum_p - logw[:, sl])
        cum_end = cum_p[c - 1:c, :]
        e_rest = jnp.exp(cum_end - cum_p)
        lhs = jnp.concatenate([_stack_heads(a_p * e_prev, lo_mask), _stack_heads(r_p * e_in, lo_mask)],
                              axis=0).astype(BF)
        rhs = jnp.concatenate([_stack_heads(b_p * e_out, lo_mask), _stack_heads(k_p * e_out, lo_mask)],
                              axis=0).astype(BF)
        scores = _dot_nt(lhs, rhs)
        ab = jnp.where(strict, scores[:2 * c, :2 * c], 0.0)
        ak = jnp.where(strict, scores[:2 * c, 2 * c:], 0.0)
        rbk = jnp.concatenate([jnp.where(incl, scores[2 * c:, :2 * c], 0.0),
                               jnp.where(incl, scores[2 * c:, 2 * c:], 0.0)], axis=1).astype(BF)

        n = jnp.where(blk_xor == 1, ab, 0.0)
        blk = 2
        while blk < c:
            off = jnp.where((blk_xor & -blk) == blk, ab, 0.0)
            n_b = n.astype(BF)
            x = off + _dot(off.astype(BF), n_b)
            n = n + x + _dot(n_b, x.astype(BF))
            blk *= 2

        s0 = s_scr[p]
        ls = _dot_nt(lhs, s0.astype(BF))
        vs = _stack_heads(v_p, lo_mask)
        zz = ls[:2 * c] + _dot(ak.astype(BF), vs.astype(BF))
        u = zz + _dot(n.astype(BF), zz.astype(BF))
        uv = jnp.concatenate([u, vs], axis=0).astype(BF)
        y = ls[2 * c:] + _dot(rbk, uv)
        ys.append(y[:c] + y[c:])
        bk_rest = jnp.concatenate([_stack_heads(b_p * e_rest, lo_mask), _stack_heads(k_p * e_rest, lo_mask)],
                                  axis=0).astype(BF)
        s_scr[p] = s0 * jnp.exp(cum_end) + _dot_tn(uv, bk_rest)

    outs = []
    for p in range(4):
        sl = slice(p * PAIR, (p + 1) * PAIR)
        y = ys[p]
        mean = _head_sum(y, ones_blk) * (1.0 / HEAD_DIM)
        dlt = y - mean
        var = _head_sum(dlt * dlt, ones_blk) * (1.0 / HEAD_DIM)
        yn = dlt * lax.rsqrt(var + A_GN_EPS) * ln_w[:, sl] + ln_b[:, sl]
        bonus = _head_sum(r[:, sl] * k[:, sl] * r_k[:, sl], ones_blk)
        outs.append((yn + bonus * v[:, sl]) * g[:, sl])
    o_ref[0] = jnp.concatenate(outs, axis=1).astype(o_ref.dtype)


def rwkv7_mix(pa, mu, wwa, g2, vec):
    b, s, cols = pa.shape
    c = WKV_CHUNK
    width = 4 * PAIR
    return pl.pallas_call(
        _rwkv_kernel,
        grid=(b, s // c),
        in_specs=[pl.BlockSpec((1, c, cols), lambda i, j: (i, j, 0)),
                  pl.BlockSpec((1, 8, cols), lambda i, j: (i, jnp.maximum(j * (c // 8) - 1, 0), 0)),
                  _resident((1, cols)), _resident(wwa.shape), _resident(g2.shape), _resident(vec.shape)],
        out_specs=pl.BlockSpec((1, c, width), lambda i, j: (i, j, 0)),
        out_shape=jax.ShapeDtypeStruct((b, s, width), BF),
        scratch_shapes=[pltpu.VMEM((4, PAIR, PAIR), F32)],
        compiler_params=_params("arbitrary", "arbitrary"),
        name="rwkv7_mix",
    )(pa, pa, mu.reshape(1, cols), wwa, g2, vec)


def _retention_kernel(pb_ref, rope_ref, tab_ref, dm_ref, lnw_ref, o_ref, s_scr):
    ci, bi = pl.program_id(0), pl.program_id(1)
    c = pb_ref.shape[1]
    w = 4 * PAIR

    @pl.when(ci == 0)
    def _():
        s_scr[bi] = jnp.zeros(s_scr.shape[1:], F32)

    lane = lax.broadcasted_iota(jnp.int32, (c, w), 1)
    first_half = (lane % HEAD_DIM) < (HEAD_DIM // 2)
    cos, sin = rope_ref[:, :w], rope_ref[:, w:]

    def rotary(t):
        swapped = jnp.where(first_half, pltpu.roll(t, w - HEAD_DIM // 2, axis=1), pltpu.roll(t, HEAD_DIM // 2, axis=1))
        return t * cos + swapped * sin

    q = rotary(pb_ref[0, :, 0:w].astype(F32))
    k = rotary(pb_ref[0, :, w:2 * w].astype(F32)) * (HEAD_DIM ** -0.5)
    v = pb_ref[0, :, 2 * w:3 * w].astype(F32)
    g = pb_ref[0, :, 3 * w:4 * w].astype(F32)
    xi, zeta, cdecay = tab_ref[0], tab_ref[1], tab_ref[2][0:1, :]

    lo_mask, ones_blk = _pair_consts(c)
    outs = []
    for p in range(4):
        sl = slice(p * PAIR, (p + 1) * PAIR)
        q_p, k_p, v_p = q[:, sl], k[:, sl], v[:, sl]
        k_b = k_p.astype(BF)
        s_lo = _dot_nt(jnp.where(lo_mask, q_p, 0.0).astype(BF), k_b) * dm_ref[2 * p]
        s_hi = _dot_nt(jnp.where(lo_mask, 0.0, q_p).astype(BF), k_b) * dm_ref[2 * p + 1]
        vs = _stack_heads(v_p, lo_mask).astype(BF)
        inner = _dot(jnp.concatenate([s_lo, s_hi], axis=1).astype(BF), vs)
        st = s_scr[bi, p]
        cross = _dot((q_p * xi[:, sl]).astype(BF), st.astype(BF))
        ks = _stack_heads(k_p * zeta[:, sl], lo_mask).astype(BF)
        s_scr[bi, p] = st * cdecay[:, sl] + _dot_tn(ks, vs)
        o = inner + cross
        ms = _head_sum(o * o, ones_blk) * (1.0 / HEAD_DIM)
        outs.append(o * lax.rsqrt(ms + NORM_EPS))
    o = jnp.concatenate(outs, axis=1) * lnw_ref[...]
    o_ref[0] = (o * (g * jax.nn.sigmoid(g))).astype(o_ref.dtype)


def _retention_tables(s, c, n_heads):
    d = HEAD_DIM
    inv = ROPE_BASE ** (-jnp.arange(0, d, 2, dtype=F32) / d)
    ang = jnp.arange(s, dtype=F32)[:, None] * inv[None, :]
    cos, sin = jnp.cos(ang), jnp.sin(ang)
    cos_full = jnp.tile(jnp.concatenate([cos, cos], axis=1), (1, n_heads))
    sin_signed = jnp.tile(jnp.concatenate([-sin, sin], axis=1), (1, n_heads))
    rope = jnp.concatenate([cos_full, sin_signed], axis=1)
    gamma = 1.0 - 2.0 ** (-5.0 - jnp.arange(n_heads, dtype=F32))
    lg = jnp.log(gamma)[:, None]
    idx = jnp.arange(c, dtype=F32)
    rel = idx[:, None] - idx[None, :]
    dmat = jnp.where(rel >= 0, jnp.exp(jnp.maximum(rel, 0.0)[None] * lg[..., None]), 0.0)
    expand = lambda t: jnp.repeat(t.T, d, axis=1)
    zeta = expand(jnp.exp((c - 1 - idx)[None, :] * lg))
    xi = expand(jnp.exp((idx + 1)[None, :] * lg))
    cdecay = expand(jnp.broadcast_to(jnp.exp(c * lg), (n_heads, c)))
    return rope, jnp.stack([xi, zeta, cdecay]), dmat


def retention_mix(pb, ln_w):
    b, s, cols = pb.shape
    c = RET_CHUNK
    width = cols // 4
    rope, tab, dmat = _retention_tables(s, c, width // HEAD_DIM)
    return pl.pallas_call(
        _retention_kernel,
        grid=(s // c, b),
        in_specs=[pl.BlockSpec((1, c, cols), lambda j, i: (i, j, 0)),
                  pl.BlockSpec((c, 2 * width), lambda j, i: (j, 0)),
                  _resident(tab.shape), _resident(dmat.shape), _resident((1, width))],
        out_specs=pl.BlockSpec((1, c, width), lambda j, i: (i, j, 0)),
        out_shape=jax.ShapeDtypeStruct((b, s, width), BF),
        scratch_shapes=[pltpu.VMEM((b, 4, PAIR, PAIR), F32)],
        compiler_params=_params("arbitrary", "arbitrary"),
        name="retention_mix",
    )(pb, rope, tab, dmat, ln_w.reshape(1, width))


def _t5_bucket(rel):
    n = jnp.maximum(rel, 0)
    max_exact = REL_BUCKETS // 2
    large = max_exact + (jnp.log(jnp.maximum(n, max_exact).astype(F32) / max_exact)
                         / math.log(REL_MAX_DIST / max_exact) * (REL_BUCKETS - max_exact)).astype(jnp.int32)
    large = jnp.minimum(large, REL_BUCKETS - 1)
    return jnp.where(n < max_exact, n, large)


def _diff_attn_kernel(relb_ref, lqk_ref, q_ref, k_ref, v_ref, bkt_ref, lnw_ref, o_ref,
                      bias_scr, m_scr, l_scr, acc_scr, *, lam_init):
    hi, qi = pl.program_id(1), pl.program_id(2)
    t = q_ref.shape[1]
    far_bias = relb_ref[REL_BUCKETS - 1, hi]

    @pl.when(qi == 0)
    def _():
        for tile in range(2):
            bkt = bkt_ref[tile]
            bias = jnp.full((t, t), MASK_VALUE, F32)
            for bucket in range(REL_BUCKETS):
                bias = jnp.where(bkt == bucket, relb_ref[bucket, hi] - far_bias, bias)
            bias_scr[tile] = bias

    lane = lax.broadcasted_iota(jnp.int32, (t, PAIR), 1)
    q = q_ref[0] * (HEAD_DIM ** -0.5)
    q_halves = (jnp.where(lane < HEAD_DIM, q, 0.0).astype(BF), jnp.where(lane < HEAD_DIM, 0.0, q).astype(BF))

    m_scr[...] = jnp.full(m_scr.shape, MASK_VALUE, F32)
    l_scr[...] = jnp.zeros(l_scr.shape, F32)
    acc_scr[...] = jnp.zeros(acc_scr.shape, F32)

    def update(kj, bias):
        start = pl.multiple_of(kj * t, t)
        k_t = k_ref[0, pl.ds(start, t), :]
        v_t = v_ref[0, pl.ds(start, t), :]
        for c in range(2):
            s = _dot_nt(q_halves[c], k_t)
            if bias is not None:
                s = s + bias
            m_old = m_scr[c]
            m_new = jnp.maximum(m_old, jnp.max(s, axis=-1, keepdims=True))
            alpha = jnp.exp(m_old - m_new)
            p = jnp.exp(s - m_new)
            l_scr[c] = alpha * l_scr[c] + jnp.sum(p, axis=-1, keepdims=True)
            acc_scr[c] = alpha * acc_scr[c] + _dot(p.astype(BF), v_t)
            m_scr[c] = m_new

    update(qi, bias_scr[0])

    @pl.when(qi >= 1)
    def _():
        update(qi - 1, bias_scr[1])

    def far_body(kj, carry):
        update(kj, None)
        return carry

    lax.fori_loop(0, jnp.maximum(qi - 1, 0), far_body, 0)

    lam = (jnp.exp(jnp.sum(lqk_ref[0:1, :] * lqk_ref[1:2, :], axis=-1, keepdims=True))
           - jnp.exp(jnp.sum(lqk_ref[2:3, :] * lqk_ref[3:4, :], axis=-1, keepdims=True)) + lam_init)
    o = acc_scr[0] / l_scr[0] - lam * (acc_scr[1] / l_scr[1])
    o_ref[0] = (_rms(o, lnw_ref[...]) * (1.0 - lam_init)).astype(o_ref.dtype)


def diff_attention(qkv, rel_bias, lqk, ln_w, lam_init):
    b, s, cols = qkv.shape
    d = cols // 3
    n_heads = d // PAIR
    t = ATTN_TILE
    pos = jnp.arange(t, dtype=jnp.int32)
    rel = pos[:, None] - pos[None, :]
    buckets = jnp.stack([jnp.where(rel >= 0, _t5_bucket(rel), -1), _t5_bucket(rel + t)])
    return pl.pallas_call(
        functools.partial(_diff_attn_kernel, lam_init=lam_init),
        grid=(b, n_heads, s // t),
        in_specs=[pl.BlockSpec(memory_space=pltpu.SMEM),
                  _resident(lqk.shape),
                  pl.BlockSpec((1, t, PAIR), lambda i, h, j: (i, j, h)),
                  pl.BlockSpec((1, s, PAIR), lambda i, h, j: (i, 0, n_heads + h)),
                  pl.BlockSpec((1, s, PAIR), lambda i, h, j: (i, 0, 2 * n_heads + h)),
                  _resident(buckets.shape), _resident((1, PAIR))],
        out_specs=pl.BlockSpec((1, t, PAIR), lambda i, h, j: (i, j, h)),
        out_shape=jax.ShapeDtypeStruct((b, s, d), BF),
        scratch_shapes=[pltpu.VMEM((2, t, t), F32), pltpu.VMEM((2, t, 1), F32), pltpu.VMEM((2, t, 1), F32),
                        pltpu.VMEM((2, t, PAIR), F32)],
        compiler_params=_params("arbitrary", "arbitrary", "arbitrary"),
        name="diff_attention",
    )(rel_bias, lqk, qkv, qkv, qkv, buckets, ln_w.reshape(1, PAIR))


def kernel(x, mem, rel_bias, mem_norm_w, final_norm_w, norm_mix_w, norm_cross_w, norm_mlp_w, xattn_w_q, xattn_w_kv, xattn_w_o, mlp_w1, mlp_w2, hyb_w_in, rwkv_mu, rwkv_w0, rwkv_w2, rwkv_a0, rwkv_a2, rwkv_g2, rwkv_k_k, rwkv_k_a, rwkv_r_k, rwkv_ln_w, rwkv_ln_b, ret_ln_w, hyb_w_out, diff_w_in, diff_lq1, diff_lk1, diff_lq2, diff_lk2, diff_ln_w, diff_w_out):
    b, s, d = x.shape
    n_mem = mem.shape[1]
    depth = norm_mix_w.shape[0]
    a_width = rwkv_w0.shape[1]
    a_cols = rwkv_mu.shape[1]
    m = b * s
    bf = lambda t: t.astype(BF)

    h = x.reshape(m, d)
    mem2 = mem.reshape(b * n_mem, d)
    for layer in range(depth):
        i = layer // 2
        if layer % 2 == 0:
            pa, pb = norm_matmul(h, norm_mix_w[layer], bf(hyb_w_in[i]),
                                 (a_cols, hyb_w_in.shape[2] - a_cols), (F32, BF))
            lora = rwkv_w2.shape[1]
            zeros = jnp.zeros((lora, a_width), F32)
            wwa = bf(jnp.concatenate([jnp.concatenate([rwkv_w2[i], zeros], axis=1),
                                      jnp.concatenate([zeros, rwkv_a2[i]], axis=1)], axis=0))
            vec = jnp.stack([rwkv_w0[i], rwkv_a0[i], rwkv_k_k[i], rwkv_k_a[i], rwkv_r_k[i].reshape(-1),
                             rwkv_ln_w[i], rwkv_ln_b[i], jnp.zeros((a_width,), F32)])
            y_a = rwkv7_mix(pa.reshape(b, s, -1), rwkv_mu[i], wwa, bf(rwkv_g2[i]), vec)
            y_b = retention_mix(pb.reshape(b, s, -1), ret_ln_w[i])
            w_out = bf(hyb_w_out[i])
            h = matmul_residual([y_a.reshape(m, -1), y_b.reshape(m, -1)], [w_out[:a_width], w_out[a_width:]], h)
        else:
            lam_init = 0.8 - 0.6 * math.exp(-0.3 * layer)
            (qkv,) = norm_matmul(h, norm_mix_w[layer], bf(diff_w_in[i]), (3 * d,), (BF,))
            lqk = jnp.stack([diff_lq1[i], diff_lk1[i], diff_lq2[i], diff_lk2[i]])
            o = diff_attention(qkv.reshape(b, s, -1), rel_bias, lqk, diff_ln_w[i], lam_init)
            h = matmul_residual([o.reshape(m, -1)], [bf(diff_w_out[i])], h)
        (kv,) = norm_matmul(mem2, mem_norm_w, bf(xattn_w_kv[layer]), (2 * d,), (BF,))
        h = xattn_block(h.reshape(b, s, d), norm_cross_w[layer], bf(xattn_w_q[layer]),
                        kv.reshape(b, n_mem, 2 * d), bf(xattn_w_o[layer])).reshape(m, d)
        h = mlp_block(h, norm_mlp_w[layer], bf(mlp_w1[layer]), bf(mlp_w2[layer]), final_norm_w,
                      final_norm=(layer == depth - 1))
    return h.reshape(b, s, d)
```

```python
import functools
import math

import jax
import jax.numpy as jnp
from jax import lax
from jax.experimental import pallas as pl
from jax.experimental.pallas import tpu as pltpu

BF = jnp.bfloat16
F32 = jnp.float32

NORM_EPS = 1e-6
HEAD_DIM = 64
PAIR = 2 * HEAD_DIM
A_GN_EPS = 64e-5
ROPE_BASE = 10000.0
REL_BUCKETS = 32
REL_MAX_DIST = 128
X_HEADS = 4
WKV_CHUNK = 64
RET_CHUNK = 128
ATTN_TILE = 512
MASK_VALUE = -1e30
VMEM_LIMIT_BYTES = 56 * 1024 * 1024


def _dot(a, b):
    return jnp.dot(a, b, preferred_element_type=F32)


def _dot_nt(a, b):
    return lax.dot_general(a, b, (((1,), (1,)), ((), ())), preferred_element_type=F32)


def _dot_tn(a, b):
    return lax.dot_general(a, b, (((0,), (0,)), ((), ())), preferred_element_type=F32)


def _rms(x, g):
    ms = jnp.mean(x * x, axis=-1, keepdims=True)
    return x * lax.rsqrt(ms + NORM_EPS) * g


def _split_bf16(x, parts):
    out = []
    for _ in range(parts):
        hi = x.astype(BF)
        out.append(hi)
        x = x - hi.astype(F32)
    return out


def _head_sum(x, ones_blk):
    return sum(_dot(part, ones_blk) for part in _split_bf16(x, 2))


def _pair_consts(rows):
    lane = lax.broadcasted_iota(jnp.int32, (rows, PAIR), 1)
    lo_mask = lane < HEAD_DIM
    r = lax.broadcasted_iota(jnp.int32, (PAIR, PAIR), 0)
    c = lax.broadcasted_iota(jnp.int32, (PAIR, PAIR), 1)
    ones_blk = jnp.where((r < HEAD_DIM) == (c < HEAD_DIM), 1.0, 0.0).astype(BF)
    return lo_mask, ones_blk


def _stack_heads(x, lo_mask):
    return jnp.concatenate([jnp.where(lo_mask, x, 0.0), jnp.where(lo_mask, 0.0, x)], axis=0)


def _params(*sem):
    return pltpu.CompilerParams(dimension_semantics=sem, vmem_limit_bytes=VMEM_LIMIT_BYTES)


def _resident(shape):
    nd = len(shape)
    return pl.BlockSpec(shape, lambda *_: (0,) * nd, pipeline_mode=pl.Buffered(1))


def _norm_matmul_kernel(x_ref, g_ref, w_ref, *o_refs, n_chunk):
    xn = _rms(x_ref[...], g_ref[...]).astype(BF)
    col = 0
    for o_ref in o_refs:
        n = o_ref.shape[-1]
        for c in range(0, n, n_chunk):
            cc = min(n_chunk, n - c)
            o_ref[:, c:c + cc] = _dot(xn, w_ref[:, col + c:col + c + cc]).astype(o_ref.dtype)
        col += n


def norm_matmul(x, g, w, splits, dtypes, tm=512, n_chunk=512):
    m, k = x.shape
    tm = min(tm, m)
    assert m % tm == 0 and sum(splits) == w.shape[1]
    return pl.pallas_call(
        functools.partial(_norm_matmul_kernel, n_chunk=n_chunk),
        grid=(m // tm,),
        in_specs=[pl.BlockSpec((tm, k), lambda i: (i, 0)), _resident((1, k)), _resident(w.shape)],
        out_specs=[pl.BlockSpec((tm, n), lambda i: (i, 0)) for n in splits],
        out_shape=[jax.ShapeDtypeStruct((m, n), dt) for n, dt in zip(splits, dtypes)],
        compiler_params=_params("parallel"),
        name="norm_matmul",
    )(x, g.reshape(1, k), w)


def _matmul_res_kernel(*refs, n_in):
    y_refs, w_refs, res_ref, o_ref = refs[:n_in], refs[n_in:2 * n_in], refs[2 * n_in], refs[2 * n_in + 1]
    acc = res_ref[...]
    for y_ref, w_ref in zip(y_refs, w_refs):
        acc = acc + _dot(y_ref[...], w_ref[...])
    o_ref[...] = acc


def matmul_residual(ys, ws, res, tm=512):
    m, n = res.shape
    tm = min(tm, m)
    n_in = len(ys)
    return pl.pallas_call(
        functools.partial(_matmul_res_kernel, n_in=n_in),
        grid=(m // tm,),
        in_specs=([pl.BlockSpec((tm, y.shape[1]), lambda i: (i, 0)) for y in ys]
                  + [_resident(w.shape) for w in ws]
                  + [pl.BlockSpec((tm, n), lambda i: (i, 0))]),
        out_specs=pl.BlockSpec((tm, n), lambda i: (i, 0)),
        out_shape=jax.ShapeDtypeStruct((m, n), F32),
        compiler_params=_params("parallel"),
        name="matmul_residual",
    )(*ys, *ws, res)


def _mlp_kernel(h_ref, g_ref, w1_ref, w2_ref, gf_ref, o_ref, a_scr, *, f_chunk, final_norm):
    h = h_ref[...]
    xn = _rms(h, g_ref[...]).astype(BF)
    for c in range(0, a_scr.shape[1], f_chunk):
        a = jnp.maximum(_dot(xn, w1_ref[:, c:c + f_chunk]), 0.0)
        a_scr[:, c:c + f_chunk] = (a * a).astype(BF)
    out = h + _dot(a_scr[...], w2_ref[...])
    if final_norm:
        out = _rms(out, gf_ref[...])
    o_ref[...] = out


def mlp_block(h, g, w1, w2, g_final, final_norm, tm=512, f_chunk=512):
    m, d = h.shape
    tm = min(tm, m)
    f = w1.shape[1]
    return pl.pallas_call(
        functools.partial(_mlp_kernel, f_chunk=f_chunk, final_norm=final_norm),
        grid=(m // tm,),
        in_specs=[pl.BlockSpec((tm, d), lambda i: (i, 0)), _resident((1, d)), _resident(w1.shape),
                  _resident(w2.shape), _resident((1, d))],
        out_specs=pl.BlockSpec((tm, d), lambda i: (i, 0)),
        out_shape=jax.ShapeDtypeStruct((m, d), F32),
        scratch_shapes=[pltpu.VMEM((tm, f), BF)],
        compiler_params=_params("parallel"),
        name="mlp_block",
    )(h, g.reshape(1, d), w1, w2, g_final.reshape(1, d))


def _xattn_kernel(h_ref, g_ref, wq_ref, kv_ref, wo_ref, o_ref, a_scr):
    h = h_ref[0]
    d = h.shape[-1]
    hd = d // X_HEADS
    xn = _rms(h, g_ref[...]).astype(BF)
    q = (_dot(xn, wq_ref[...]) * (hd ** -0.5)).astype(BF)
    for i in range(X_HEADS):
        k = kv_ref[0, :, i * hd:(i + 1) * hd]
        v = kv_ref[0, :, d + i * hd:d + (i + 1) * hd]
        s = _dot_nt(q[:, i * hd:(i + 1) * hd], k)
        p = jnp.exp(s - jnp.max(s, axis=-1, keepdims=True))
        o = _dot(p.astype(BF), v) / jnp.sum(p, axis=-1, keepdims=True)
        a_scr[:, i * hd:(i + 1) * hd] = o.astype(BF)
    o_ref[0] = h + _dot(a_scr[...], wo_ref[...])


def xattn_block(h, g, wq, kv, wo, tm=512):
    b, s, d = h.shape
    tm = min(tm, s)
    return pl.pallas_call(
        _xattn_kernel,
        grid=(b, s // tm),
        in_specs=[pl.BlockSpec((1, tm, d), lambda i, j: (i, j, 0)), _resident((1, d)), _resident(wq.shape),
                  pl.BlockSpec((1,) + kv.shape[1:], lambda i, j: (i, 0, 0)), _resident(wo.shape)],
        out_specs=pl.BlockSpec((1, tm, d), lambda i, j: (i, j, 0)),
        out_shape=jax.ShapeDtypeStruct((b, s, d), F32),
        scratch_shapes=[pltpu.VMEM((tm, d), BF)],
        compiler_params=_params("parallel", "parallel"),
        name="xattn_block",
    )(h, g.reshape(1, d), wq, kv, wo)


def _rwkv_kernel(pa_ref, prev_ref, mu_ref, wwa_ref, g2_ref, vec_ref, o_ref, s_scr):
    ci = pl.program_id(1)
    c = pa_ref.shape[1]
    w = 4 * PAIR

    @pl.when(ci == 0)
    def _():
        s_scr[...] = jnp.zeros_like(s_scr)

    pa = pa_ref[0]
    prev_last = jnp.where(ci == 0, 0.0, prev_ref[0][7:8, :])
    row = lax.broadcasted_iota(jnp.int32, pa.shape, 0)
    shifted = jnp.where(row == 0, prev_last, pltpu.roll(pa, 1, axis=0))
    x = pa + (shifted - pa) * mu_ref[...]

    r, k, v = x[:, 0:w], x[:, w:2 * w], x[:, 2 * w:3 * w]
    wa, gd = x[:, 3 * w:3 * w + PAIR], x[:, 3 * w + PAIR:3 * w + 2 * PAIR]
    w0, a0, k_k, k_a, r_k, ln_w, ln_b = (vec_ref[i:i + 1, :] for i in range(7))

    lo_mask, ones_blk = _pair_consts(c)
    z = jnp.where(lo_mask, jnp.tanh(wa), wa).astype(BF)
    twa = _dot(z, wwa_ref[...])
    neg = -(w0 + twa[:, :w])
    softplus = jnp.maximum(neg, 0.0) + jnp.log1p(jnp.exp(-jnp.abs(neg)))
    logw = -jnp.exp(-softplus - 0.5)
    a = jax.nn.sigmoid(a0 + twa[:, w:])
    g = _dot(jax.nn.sigmoid(gd).astype(BF), g2_ref[...])

    kk = k * k_k
    kk_sq = kk * kk
    ss = jnp.concatenate([_head_sum(kk_sq[:, p * PAIR:(p + 1) * PAIR], ones_blk) for p in range(4)], axis=1)
    kk = kk / jnp.maximum(jnp.sqrt(ss), 1e-12)
    k = k * (1.0 + (a - 1.0) * k_a)
    aa = -kk
    bb = kk * a

    tr = lax.broadcasted_iota(jnp.int32, (c, c), 0)
    tc = lax.broadcasted_iota(jnp.int32, (c, c), 1)
    tril = jnp.where(tr >= tc, 1.0, 0.0).astype(BF)
    cum = sum(_dot(tril, part) for part in _split_bf16(logw, 3))

    sr = lax.broadcasted_iota(jnp.int32, (2 * c, 2 * c), 0)
    sc = lax.broadcasted_iota(jnp.int32, (2 * c, 2 * c), 1)
    blk_xor = sr ^ sc
    strict, incl = sr % c > sc % c, sr % c >= sc % c

    ys = []
    for p in range(4):
        sl = slice(p * PAIR, (p + 1) * PAIR)
        cum_p, r_p, k_p, v_p, a_p, b_p = cum[:, sl], r[:, sl], k[:, sl], v[:, sl], aa[:, sl], bb[:, sl]
        e_in = jnp.exp(cum_p)
        e_out = jnp.exp(-cum_p)
        e_prev = jnp.exp(cum_p - logw[:, sl])
        cum_end = cum_p[c - 1:c, :]
        e_rest = jnp.exp(cum_end - cum_p)
        lhs = jnp.concatenate([_stack_heads(a_p * e_prev, lo_mask), _stack_heads(r_p * e_in, lo_mask)],
                              axis=0).astype(BF)
        rhs = jnp.concatenate([_stack_heads(b_p * e_out, lo_mask), _stack_heads(k_p * e_out, lo_mask)],
                              axis=0).astype(BF)
        scores = _dot_nt(lhs, rhs)
        ab = jnp.where(strict, scores[:2 * c, :2 * c], 0.0)
        ak = jnp.where(strict, scores[:2 * c, 2 * c:], 0.0)
        rbk = jnp.concatenate([jnp.where(incl, scores[2 * c:, :2 * c], 0.0),
                               jnp.where(incl, scores[2 * c:, 2 * c:], 0.0)], axis=1).astype(BF)

        n = jnp.where(blk_xor == 1, ab, 0.0)
        blk = 2
        while blk < c:
            off = jnp.where((blk_xor & -blk) == blk, ab, 0.0)
            n_b = n.astype(BF)
            x = off + _dot(off.astype(BF), n_b)
            n = n + x + _dot(n_b, x.astype(BF))
            blk *= 2

        s0 = s_scr[p]
        ls = _dot_nt(lhs, s0.astype(BF))
        vs = _stack_heads(v_p, lo_mask)
        zz = ls[:2 * c] + _dot(ak.astype(BF), vs.astype(BF))
        u = zz + _dot(n.astype(BF), zz.astype(BF))
        uv = jnp.concatenate([u, vs], axis=0).astype(BF)
        y = ls[2 * c:] + _dot(rbk, uv)
        ys.append(y[:c] + y[c:])
        bk_rest = jnp.concatenate([_stack_heads(b_p * e_rest, lo_mask), _stack_heads(k_p * e_rest, lo_mask)],
                                  axis=0).astype(BF)
        s_scr[p] = s0 * jnp.exp(cum_end) + _dot_tn(uv, bk_rest)

    outs = []
    for p in range(4):
        sl = slice(p * PAIR, (p + 1) * PAIR)
        y = ys[p]
        mean = _head_sum(y, ones_blk) * (1.0 / HEAD_DIM)
        dlt = y - mean
        var = _head_sum(dlt * dlt, ones_blk) * (1.0 / HEAD_DIM)
        yn = dlt * lax.rsqrt(var + A_GN_EPS) * ln_w[:, sl] + ln_b[:, sl]
        bonus = _head_sum(r[:, sl] * k[:, sl] * r_k[:, sl], ones_blk)
        outs.append((yn + bonus * v[:, sl]) * g[:, sl])
    o_ref[0] = jnp.concatenate(outs, axis=1).astype(o_ref.dtype)


def rwkv7_mix(pa, mu, wwa, g2, vec):
    b, s, cols = pa.shape
    c = WKV_CHUNK
    width = 4 * PAIR
    return pl.pallas_call(
        _rwkv_kernel,
        grid=(b, s // c),
        in_specs=[pl.BlockSpec((1, c, cols), lambda i, j: (i, j, 0)),
                  pl.BlockSpec((1, 8, cols), lambda i, j: (i, jnp.maximum(j * (c // 8) - 1, 0), 0)),
                  _resident((1, cols)), _resident(wwa.shape), _resident(g2.shape), _resident(vec.shape)],
        out_specs=pl.BlockSpec((1, c, width), lambda i, j: (i, j, 0)),
        out_shape=jax.ShapeDtypeStruct((b, s, width), BF),
        scratch_shapes=[pltpu.VMEM((4, PAIR, PAIR), F32)],
        compiler_params=_params("arbitrary", "arbitrary"),
        name="rwkv7_mix",
    )(pa, pa, mu.reshape(1, cols), wwa, g2, vec)


def _retention_kernel(pb_ref, rope_ref, tab_ref, dm_ref, lnw_ref, o_ref, s_scr):
    ci, bi = pl.program_id(0), pl.program_id(1)
    c = pb_ref.shape[1]
    w = 4 * PAIR

    @pl.when(ci == 0)
    def _():
        s_scr[bi] = jnp.zeros(s_scr.shape[1:], F32)

    lane = lax.broadcasted_iota(jnp.int32, (c, w), 1)
    first_half = (lane % HEAD_DIM) < (HEAD_DIM // 2)
    cos, sin = rope_ref[:, :w], rope_ref[:, w:]

    def rotary(t):
        swapped = jnp.where(first_half, pltpu.roll(t, w - HEAD_DIM // 2, axis=1), pltpu.roll(t, HEAD_DIM // 2, axis=1))
        return t * cos + swapped * sin

    q = rotary(pb_ref[0, :, 0:w].astype(F32))
    k = rotary(pb_ref[0, :, w:2 * w].astype(F32)) * (HEAD_DIM ** -0.5)
    v = pb_ref[0, :, 2 * w:3 * w].astype(F32)
    g = pb_ref[0, :, 3 * w:4 * w].astype(F32)
    xi, zeta, cdecay = tab_ref[0], tab_ref[1], tab_ref[2][0:1, :]

    lo_mask, ones_blk = _pair_consts(c)
    outs = []
    for p in range(4):
        sl = slice(p * PAIR, (p + 1) * PAIR)
        q_p, k_p, v_p = q[:, sl], k[:, sl], v[:, sl]
        k_b = k_p.astype(BF)
        s_lo = _dot_nt(jnp.where(lo_mask, q_p, 0.0).astype(BF), k_b) * dm_ref[2 * p]
        s_hi = _dot_nt(jnp.where(lo_mask, 0.0, q_p).astype(BF), k_b) * dm_ref[2 * p + 1]
        vs = _stack_heads(v_p, lo_mask).astype(BF)
        inner = _dot(jnp.concatenate([s_lo, s_hi], axis=1).astype(BF), vs)
        st = s_scr[bi, p]
        cross = _dot((q_p * xi[:, sl]).astype(BF), st.astype(BF))
        ks = _stack_heads(k_p * zeta[:, sl], lo_mask).astype(BF)
        s_scr[bi, p] = st * cdecay[:, sl] + _dot_tn(ks, vs)
        o = inner + cross
        ms = _head_sum(o * o, ones_blk) * (1.0 / HEAD_DIM)
        outs.append(o * lax.rsqrt(ms + NORM_EPS))
    o = jnp.concatenate(outs, axis=1) * lnw_ref[...]
    o_ref[0] = (o * (g * jax.nn.sigmoid(g))).astype(o_ref.dtype)


def _retention_tables(s, c, n_heads):
    d = HEAD_DIM
    inv = ROPE_BASE ** (-jnp.arange(0, d, 2, dtype=F32) / d)
    ang = jnp.arange(s, dtype=F32)[:, None] * inv[None, :]
    cos, sin = jnp.cos(ang), jnp.sin(ang)
    cos_full = jnp.tile(jnp.concatenate([cos, cos], axis=1), (1, n_heads))
    sin_signed = jnp.tile(jnp.concatenate([-sin, sin], axis=1), (1, n_heads))
    rope = jnp.concatenate([cos_full, sin_signed], axis=1)
    gamma = 1.0 - 2.0 ** (-5.0 - jnp.arange(n_heads, dtype=F32))
    lg = jnp.log(gamma)[:, None]
    idx = jnp.arange(c, dtype=F32)
    rel = idx[:, None] - idx[None, :]
    dmat = jnp.where(rel >= 0, jnp.exp(jnp.maximum(rel, 0.0)[None] * lg[..., None]), 0.0)
    expand = lambda t: jnp.repeat(t.T, d, axis=1)
    zeta = expand(jnp.exp((c - 1 - idx)[None, :] * lg))
    xi = expand(jnp.exp((idx + 1)[None, :] * lg))
    cdecay = expand(jnp.broadcast_to(jnp.exp(c * lg), (n_heads, c)))
    return rope, jnp.stack([xi, zeta, cdecay]), dmat


def retention_mix(pb, ln_w):
    b, s, cols = pb.shape
    c = RET_CHUNK
    width = cols // 4
    rope, tab, dmat = _retention_tables(s, c, width // HEAD_DIM)
    return pl.pallas_call(
        _retention_kernel,
        grid=(s // c, b),
        in_specs=[pl.BlockSpec((1, c, cols), lambda j, i: (i, j, 0)),
                  pl.BlockSpec((c, 2 * width), lambda j, i: (j, 0)),
                  _resident(tab.shape), _resident(dmat.shape), _resident((1, width))],
        out_specs=pl.BlockSpec((1, c, width), lambda j, i: (i, j, 0)),
        out_shape=jax.ShapeDtypeStruct((b, s, width), BF),
        scratch_shapes=[pltpu.VMEM((b, 4, PAIR, PAIR), F32)],
        compiler_params=_params("arbitrary", "arbitrary"),
        name="retention_mix",
    )(pb, rope, tab, dmat, ln_w.reshape(1, width))


def _t5_bucket(rel):
    n = jnp.maximum(rel, 0)
    max_exact = REL_BUCKETS // 2
    large = max_exact + (jnp.log(jnp.maximum(n, max_exact).astype(F32) / max_exact)
                         / math.log(REL_MAX_DIST / max_exact) * (REL_BUCKETS - max_exact)).astype(jnp.int32)
    large = jnp.minimum(large, REL_BUCKETS - 1)
    return jnp.where(n < max_exact, n, large)


def _qkv_proj_kernel(x_ref, g_ref, wqk_ref, wvt_ref, qk_ref, vt_ref, *, n_chunk):
    xn = _rms(x_ref[...], g_ref[...]).astype(BF)
    for c in range(0, qk_ref.shape[-1], n_chunk):
        qk_ref[:, c:c + n_chunk] = _dot(xn, wqk_ref[:, c:c + n_chunk]).astype(BF)
    n_heads, n_sub, _, t = vt_ref.shape[1:]
    for h in range(n_heads):
        vt = _dot_nt(wvt_ref[h * PAIR:(h + 1) * PAIR, :], xn)
        for j in range(n_sub):
            vt_ref[0, h, j] = vt[:, j * t:(j + 1) * t].astype(BF)


def qkv_proj(x, g, w_qk, w_vt, b, tm=512, n_chunk=512):
    m, k = x.shape
    s = m // b
    tm = min(tm, s)
    t = min(ATTN_TILE, s)
    n_heads = w_vt.shape[0] // PAIR
    per_b = s // tm
    return pl.pallas_call(
        functools.partial(_qkv_proj_kernel, n_chunk=n_chunk),
        grid=(m // tm,),
        in_specs=[pl.BlockSpec((tm, k), lambda i: (i, 0)), _resident((1, k)), _resident(w_qk.shape),
                  _resident(w_vt.shape)],
        out_specs=[pl.BlockSpec((tm, w_qk.shape[1]), lambda i: (i, 0)),
                   pl.BlockSpec((1, n_heads, tm // t, PAIR, t), lambda i: (i // per_b, 0, i % per_b, 0, 0))],
        out_shape=[jax.ShapeDtypeStruct((m, w_qk.shape[1]), BF),
                   jax.ShapeDtypeStruct((b, n_heads, s // t, PAIR, t), BF)],
        compiler_params=_params("parallel"),
        name="qkv_proj",
    )(x, g.reshape(1, k), w_qk, w_vt)


def _diff_attn_kernel(relb_ref, lqk_ref, q_ref, k_ref, vt_ref, bkt_ref, lnw_ref, o_ref,
                      bias_scr, m_scr, l_scr, acc_scr, *, lam_init):
    hi, bi, qi = pl.program_id(0), pl.program_id(1), pl.program_id(2)
    t = q_ref.shape[1]
    far_bias = relb_ref[REL_BUCKETS - 1, hi]

    @pl.when((bi == 0) & (qi == 0))
    def _():
        for tile in range(2):
            bkt = bkt_ref[tile]
            bias = jnp.full((t, t), MASK_VALUE, F32)
            for bucket in range(REL_BUCKETS):
                bias = jnp.where(bkt == bucket, relb_ref[bucket, hi] - far_bias, bias)
            bias_scr[tile] = bias

    lane = lax.broadcasted_iota(jnp.int32, (t, PAIR), 1)
    q = q_ref[0] * (HEAD_DIM ** -0.5)
    q_halves = (jnp.where(lane < HEAD_DIM, q, 0.0).astype(BF), jnp.where(lane < HEAD_DIM, 0.0, q).astype(BF))

    m_scr[...] = jnp.full(m_scr.shape, MASK_VALUE, F32)
    l_scr[...] = jnp.zeros(l_scr.shape, F32)
    acc_scr[...] = jnp.zeros(acc_scr.shape, F32)

    def update(kj, bias):
        k_t = k_ref[0, pl.ds(pl.multiple_of(kj * t, t), t), :]
        v_t = vt_ref[0, 0, kj]
        for c in range(2):
            s = _dot_nt(k_t, q_halves[c])
            if bias is not None:
                s = s + bias
            m_old = m_scr[c]
            m_new = jnp.maximum(m_old, jnp.max(s, axis=0, keepdims=True))
            alpha = jnp.exp(m_old - m_new)
            p = jnp.exp(s - m_new)
            l_scr[c] = alpha * l_scr[c] + jnp.sum(p, axis=0, keepdims=True)
            acc_scr[c] = alpha * acc_scr[c] + _dot(v_t, p.astype(BF))
            m_scr[c] = m_new

    update(qi, bias_scr[0])

    @pl.when(qi >= 1)
    def _():
        update(qi - 1, bias_scr[1])

    def far_body(kj, carry):
        update(kj, None)
        return carry

    lax.fori_loop(0, jnp.maximum(qi - 1, 0), far_body, 0)

    lam = (jnp.exp(jnp.sum(lqk_ref[0:1, :] * lqk_ref[1:2, :], axis=-1, keepdims=True))
           - jnp.exp(jnp.sum(lqk_ref[2:3, :] * lqk_ref[3:4, :], axis=-1, keepdims=True)) + lam_init)
    o_t = acc_scr[0] * (1.0 / l_scr[0]) - lam * (acc_scr[1] * (1.0 / l_scr[1]))
    o_t = o_t * lax.rsqrt(jnp.mean(o_t * o_t, axis=0, keepdims=True) + NORM_EPS)
    o_ref[0] = (o_t.T * lnw_ref[...] * (1.0 - lam_init)).astype(o_ref.dtype)


def diff_attention(qk, vt, rel_bias, lqk, ln_w, lam_init):
    b, s, cols = qk.shape
    d = cols // 2
    n_heads = d // PAIR
    t = vt.shape[-1]
    pos = jnp.arange(t, dtype=jnp.int32)
    rel = pos[None, :] - pos[:, None]
    buckets = jnp.stack([jnp.where(rel >= 0, _t5_bucket(rel), -1), _t5_bucket(rel + t)])
    return pl.pallas_call(
        functools.partial(_diff_attn_kernel, lam_init=lam_init),
        grid=(n_heads, b, s // t),
        in_specs=[pl.BlockSpec(memory_space=pltpu.SMEM),
                  _resident(lqk.shape),
                  pl.BlockSpec((1, t, PAIR), lambda h, i, j: (i, j, h)),
                  pl.BlockSpec((1, s, PAIR), lambda h, i, j: (i, 0, n_heads + h)),
                  pl.BlockSpec((1, 1, s // t, PAIR, t), lambda h, i, j: (i, h, 0, 0, 0)),
                  _resident(buckets.shape), _resident((1, PAIR))],
        out_specs=pl.BlockSpec((1, t, PAIR), lambda h, i, j: (i, j, h)),
        out_shape=jax.ShapeDtypeStruct((b, s, d), BF),
        scratch_shapes=[pltpu.VMEM((2, t, t), F32), pltpu.VMEM((2, 1, t), F32), pltpu.VMEM((2, 1, t), F32),
                        pltpu.VMEM((2, PAIR, t), F32)],
        compiler_params=_params("arbitrary", "arbitrary", "arbitrary"),
        name="diff_attention",
    )(rel_bias, lqk, qk, qk, vt, buckets, ln_w.reshape(1, PAIR))


def kernel(x, mem, rel_bias, mem_norm_w, final_norm_w, norm_mix_w, norm_cross_w, norm_mlp_w, xattn_w_q, xattn_w_kv, xattn_w_o, mlp_w1, mlp_w2, hyb_w_in, rwkv_mu, rwkv_w0, rwkv_w2, rwkv_a0, rwkv_a2, rwkv_g2, rwkv_k_k, rwkv_k_a, rwkv_r_k, rwkv_ln_w, rwkv_ln_b, ret_ln_w, hyb_w_out, diff_w_in, diff_lq1, diff_lk1, diff_lq2, diff_lk2, diff_ln_w, diff_w_out):
    b, s, d = x.shape
    n_mem = mem.shape[1]
    depth = norm_mix_w.shape[0]
    a_width = rwkv_w0.shape[1]
    a_cols = rwkv_mu.shape[1]
    m = b * s
    bf = lambda t: t.astype(BF)

    h = x.reshape(m, d)
    mem2 = mem.reshape(b * n_mem, d)
    for layer in range(depth):
        i = layer // 2
        if layer % 2 == 0:
            pa, pb = norm_matmul(h, norm_mix_w[layer], bf(hyb_w_in[i]),
                                 (a_cols, hyb_w_in.shape[2] - a_cols), (F32, BF))
            lora = rwkv_w2.shape[1]
            zeros = jnp.zeros((lora, a_width), F32)
            wwa = bf(jnp.concatenate([jnp.concatenate([rwkv_w2[i], zeros], axis=1),
                                      jnp.concatenate([zeros, rwkv_a2[i]], axis=1)], axis=0))
            vec = jnp.stack([rwkv_w0[i], rwkv_a0[i], rwkv_k_k[i], rwkv_k_a[i], rwkv_r_k[i].reshape(-1),
                             rwkv_ln_w[i], rwkv_ln_b[i], jnp.zeros((a_width,), F32)])
            y_a = rwkv7_mix(pa.reshape(b, s, -1), rwkv_mu[i], wwa, bf(rwkv_g2[i]), vec)
            y_b = retention_mix(pb.reshape(b, s, -1), ret_ln_w[i])
            w_out = bf(hyb_w_out[i])
            h = matmul_residual([y_a.reshape(m, -1), y_b.reshape(m, -1)], [w_out[:a_width], w_out[a_width:]], h)
        else:
            lam_init = 0.8 - 0.6 * math.exp(-0.3 * layer)
            qk, vt = qkv_proj(h, norm_mix_w[layer], bf(diff_w_in[i][:, :2 * d]), bf(diff_w_in[i][:, 2 * d:].T), b)
            lqk = jnp.stack([diff_lq1[i], diff_lk1[i], diff_lq2[i], diff_lk2[i]])
            o = diff_attention(qk.reshape(b, s, -1), vt, rel_bias, lqk, diff_ln_w[i], lam_init)
            h = matmul_residual([o.reshape(m, -1)], [bf(diff_w_out[i])], h)
        (kv,) = norm_matmul(mem2, mem_norm_w, bf(xattn_w_kv[layer]), (2 * d,), (BF,))
        h = xattn_block(h.reshape(b, s, d), norm_cross_w[layer], bf(xattn_w_q[layer]),
                        kv.reshape(b, n_mem, 2 * d), bf(xattn_w_o[layer])).reshape(m, d)
        h = mlp_block(h, norm_mlp_w[layer], bf(mlp_w1[layer]), bf(mlp_w2[layer]), final_norm_w,
                      final_norm=(layer == depth - 1))
    return h.reshape(b, s, d)
```

```python
import functools
import math

import jax
import jax.numpy as jnp
from jax import lax
from jax.experimental import pallas as pl
from jax.experimental.pallas import tpu as pltpu

BF = jnp.bfloat16
F32 = jnp.float32

NORM_EPS = 1e-6
HEAD_DIM = 64
PAIR = 2 * HEAD_DIM
A_GN_EPS = 64e-5
ROPE_BASE = 10000.0
REL_BUCKETS = 32
REL_MAX_DIST = 128
X_HEADS = 4
WKV_CHUNK = 64
WKV_GROUP = 4
RET_CHUNK = 128
ATTN_TILE = 512
VT_ROWS = PAIR + 16
MASK_VALUE = -1e30
VMEM_LIMIT_BYTES = 56 * 1024 * 1024


def _dot(a, b):
    return jnp.dot(a, b, preferred_element_type=F32)


def _dot_nt(a, b):
    return lax.dot_general(a, b, (((1,), (1,)), ((), ())), preferred_element_type=F32)


def _dot_tn(a, b):
    return lax.dot_general(a, b, (((0,), (0,)), ((), ())), preferred_element_type=F32)


def _rms(x, g):
    ms = jnp.mean(x * x, axis=-1, keepdims=True)
    return x * lax.rsqrt(ms + NORM_EPS) * g


def _split_bf16(x, parts):
    out = []
    for _ in range(parts):
        hi = x.astype(BF)
        out.append(hi)
        x = x - hi.astype(F32)
    return out


def _head_sum(x, ones_blk):
    return sum(_dot(part, ones_blk) for part in _split_bf16(x, 2))


def _pair_consts(rows):
    lane = lax.broadcasted_iota(jnp.int32, (rows, PAIR), 1)
    lo_mask = lane < HEAD_DIM
    r = lax.broadcasted_iota(jnp.int32, (PAIR, PAIR), 0)
    c = lax.broadcasted_iota(jnp.int32, (PAIR, PAIR), 1)
    ones_blk = jnp.where((r < HEAD_DIM) == (c < HEAD_DIM), 1.0, 0.0).astype(BF)
    return lo_mask, ones_blk


def _stack_heads(x, lo_mask):
    return jnp.concatenate([jnp.where(lo_mask, x, 0.0), jnp.where(lo_mask, 0.0, x)], axis=0)


def _params(*sem):
    return pltpu.CompilerParams(dimension_semantics=sem, vmem_limit_bytes=VMEM_LIMIT_BYTES)


def _resident(shape):
    nd = len(shape)
    return pl.BlockSpec(shape, lambda *_: (0,) * nd, pipeline_mode=pl.Buffered(1))


def _norm_matmul_kernel(x_ref, g_ref, w_ref, *o_refs, n_chunk):
    xn = _rms(x_ref[...], g_ref[...]).astype(BF)
    col = 0
    for o_ref in o_refs:
        n = o_ref.shape[-1]
        for c in range(0, n, n_chunk):
            cc = min(n_chunk, n - c)
            o_ref[:, c:c + cc] = _dot(xn, w_ref[:, col + c:col + c + cc]).astype(o_ref.dtype)
        col += n


def norm_matmul(x, g, w, splits, dtypes, tm=512, n_chunk=512):
    m, k = x.shape
    tm = min(tm, m)
    assert m % tm == 0 and sum(splits) == w.shape[1]
    return pl.pallas_call(
        functools.partial(_norm_matmul_kernel, n_chunk=n_chunk),
        grid=(m // tm,),
        in_specs=[pl.BlockSpec((tm, k), lambda i: (i, 0)), _resident((1, k)), _resident(w.shape)],
        out_specs=[pl.BlockSpec((tm, n), lambda i: (i, 0)) for n in splits],
        out_shape=[jax.ShapeDtypeStruct((m, n), dt) for n, dt in zip(splits, dtypes)],
        compiler_params=_params("parallel"),
        name="norm_matmul",
    )(x, g.reshape(1, k), w)


def _matmul_res_kernel(*refs, n_in):
    y_refs, w_refs, res_ref, o_ref = refs[:n_in], refs[n_in:2 * n_in], refs[2 * n_in], refs[2 * n_in + 1]
    acc = res_ref[...]
    for y_ref, w_ref in zip(y_refs, w_refs):
        acc = acc + _dot(y_ref[...], w_ref[...])
    o_ref[...] = acc


def matmul_residual(ys, ws, res, tm=512):
    m, n = res.shape
    tm = min(tm, m)
    n_in = len(ys)
    return pl.pallas_call(
        functools.partial(_matmul_res_kernel, n_in=n_in),
        grid=(m // tm,),
        in_specs=([pl.BlockSpec((tm, y.shape[1]), lambda i: (i, 0)) for y in ys]
                  + [_resident(w.shape) for w in ws]
                  + [pl.BlockSpec((tm, n), lambda i: (i, 0))]),
        out_specs=pl.BlockSpec((tm, n), lambda i: (i, 0)),
        out_shape=jax.ShapeDtypeStruct((m, n), F32),
        compiler_params=_params("parallel"),
        name="matmul_residual",
    )(*ys, *ws, res)


def _mlp_kernel(h_ref, g_ref, w1_ref, w2_ref, gf_ref, o_ref, a_scr, *, f_chunk, final_norm):
    h = h_ref[...]
    xn = _rms(h, g_ref[...]).astype(BF)
    for c in range(0, a_scr.shape[1], f_chunk):
        a = jnp.maximum(_dot(xn, w1_ref[:, c:c + f_chunk]), 0.0)
        a_scr[:, c:c + f_chunk] = (a * a).astype(BF)
    out = h + _dot(a_scr[...], w2_ref[...])
    if final_norm:
        out = _rms(out, gf_ref[...])
    o_ref[...] = out


def mlp_block(h, g, w1, w2, g_final, final_norm, tm=512, f_chunk=512):
    m, d = h.shape
    tm = min(tm, m)
    f = w1.shape[1]
    return pl.pallas_call(
        functools.partial(_mlp_kernel, f_chunk=f_chunk, final_norm=final_norm),
        grid=(m // tm,),
        in_specs=[pl.BlockSpec((tm, d), lambda i: (i, 0)), _resident((1, d)), _resident(w1.shape),
                  _resident(w2.shape), _resident((1, d))],
        out_specs=pl.BlockSpec((tm, d), lambda i: (i, 0)),
        out_shape=jax.ShapeDtypeStruct((m, d), F32),
        scratch_shapes=[pltpu.VMEM((tm, f), BF)],
        compiler_params=_params("parallel"),
        name="mlp_block",
    )(h, g.reshape(1, d), w1, w2, g_final.reshape(1, d))


def _xattn_kernel(h_ref, g_ref, wq_ref, kv_ref, wo_ref, o_ref, a_scr):
    h = h_ref[0]
    d = h.shape[-1]
    hd = d // X_HEADS
    xn = _rms(h, g_ref[...]).astype(BF)
    q = (_dot(xn, wq_ref[...]) * (hd ** -0.5)).astype(BF)
    for i in range(X_HEADS):
        k = kv_ref[0, :, i * hd:(i + 1) * hd]
        v = kv_ref[0, :, d + i * hd:d + (i + 1) * hd]
        s = _dot_nt(q[:, i * hd:(i + 1) * hd], k)
        p = jnp.exp(s - jnp.max(s, axis=-1, keepdims=True))
        o = _dot(p.astype(BF), v) / jnp.sum(p, axis=-1, keepdims=True)
        a_scr[:, i * hd:(i + 1) * hd] = o.astype(BF)
    o_ref[0] = h + _dot(a_scr[...], wo_ref[...])


def xattn_block(h, g, wq, kv, wo, tm=512):
    b, s, d = h.shape
    tm = min(tm, s)
    return pl.pallas_call(
        _xattn_kernel,
        grid=(b, s // tm),
        in_specs=[pl.BlockSpec((1, tm, d), lambda i, j: (i, j, 0)), _resident((1, d)), _resident(wq.shape),
                  pl.BlockSpec((1,) + kv.shape[1:], lambda i, j: (i, 0, 0)), _resident(wo.shape)],
        out_specs=pl.BlockSpec((1, tm, d), lambda i, j: (i, j, 0)),
        out_shape=jax.ShapeDtypeStruct((b, s, d), F32),
        scratch_shapes=[pltpu.VMEM((tm, d), BF)],
        compiler_params=_params("parallel", "parallel"),
        name="xattn_block",
    )(h, g.reshape(1, d), wq, kv, wo)


def _bdot(a, b):
    return lax.dot_general(a, b, (((2,), (1,)), ((0,), (0,))), preferred_element_type=F32)


def _bdot_nt(a, b):
    return lax.dot_general(a, b, (((2,), (2,)), ((0,), (0,))), preferred_element_type=F32)


def _bdot_tn(a, b):
    return lax.dot_general(a, b, (((1,), (1,)), ((0,), (0,))), preferred_element_type=F32)


def _rwkv_kernel(pa_ref, prev_ref, mu_ref, wwa_ref, g2_ref, vec_ref, ones_ref, o_ref, s_scr, *, c):
    ci = pl.program_id(1)
    rows = pa_ref.shape[1]
    g = rows // c
    n_pairs = 4
    w = n_pairs * PAIR

    @pl.when(ci == 0)
    def _():
        s_scr[...] = jnp.zeros_like(s_scr)

    pa = pa_ref[0]
    prev_last = jnp.where(ci == 0, 0.0, prev_ref[0][7:8, :])
    row = lax.broadcasted_iota(jnp.int32, pa.shape, 0)
    shifted = jnp.where(row == 0, prev_last, pltpu.roll(pa, 1, axis=0))
    x = pa + (shifted - pa) * mu_ref[...]

    r, k, v = x[:, 0:w], x[:, w:2 * w], x[:, 2 * w:3 * w]
    wa, gd = x[:, 3 * w:3 * w + PAIR], x[:, 3 * w + PAIR:3 * w + 2 * PAIR]
    w0, a0, k_k, k_a, r_k, ln_w, ln_b = (vec_ref[i:i + 1, :] for i in range(7))

    lane = lax.broadcasted_iota(jnp.int32, (rows, PAIR), 1)
    ones_blk = ones_ref[...]

    def head_sum(t):
        parts = _split_bf16(t, 2)
        return jnp.concatenate([sum(_dot(part[:, p * PAIR:(p + 1) * PAIR], ones_blk) for part in parts)
                                for p in range(n_pairs)], axis=1)

    z = jnp.where(lane < HEAD_DIM, jnp.tanh(wa), wa).astype(BF)
    twa = _dot(z, wwa_ref[...])
    neg = -(w0 + twa[:, :w])
    softplus = jnp.maximum(neg, 0.0) + jnp.log1p(jnp.exp(-jnp.abs(neg)))
    logw = -jnp.exp(-softplus - 0.5)
    a = jax.nn.sigmoid(a0 + twa[:, w:])
    gate = _dot(jax.nn.sigmoid(gd).astype(BF), g2_ref[...])

    kk = k * k_k
    kk = kk / jnp.maximum(jnp.sqrt(head_sum(kk * kk)), 1e-12)
    k = k * (1.0 + (a - 1.0) * k_a)
    aa = -kk
    bb = kk * a

    tr = lax.broadcasted_iota(jnp.int32, (rows, rows), 0)
    tc = lax.broadcasted_iota(jnp.int32, (rows, rows), 1)
    tril = jnp.where((tr >= tc) & (tr // c == tc // c), 1.0, 0.0).astype(BF)
    cum = sum(_dot(tril, part) for part in _split_bf16(logw, 3))

    def to_pairs(t):
        return jnp.stack([t[gi * c:(gi + 1) * c, p * PAIR:(p + 1) * PAIR] for gi in range(g) for p in range(n_pairs)])

    def stack_heads(t):
        lo = lax.broadcasted_iota(jnp.int32, t.shape, 2) < HEAD_DIM
        return jnp.concatenate([jnp.where(lo, t, 0.0), jnp.where(lo, 0.0, t)], axis=1)

    cum_p, logw_p, r_p, k_p, v_p, a_p, b_p = (to_pairs(t) for t in (cum, logw, r, k, v, aa, bb))
    e_in = jnp.exp(cum_p)
    e_out = jnp.exp(-cum_p)
    e_prev = jnp.exp(cum_p - logw_p)
    cum_end = cum_p[:, c - 1:c, :]
    e_rest = jnp.exp(cum_end - cum_p)
    lhs = jnp.concatenate([stack_heads(a_p * e_prev), stack_heads(r_p * e_in)], axis=1).astype(BF)
    rhs = jnp.concatenate([stack_heads(b_p * e_out), stack_heads(k_p * e_out)], axis=1).astype(BF)
    bk_rest = jnp.concatenate([stack_heads(b_p * e_rest), stack_heads(k_p * e_rest)], axis=1).astype(BF)
    vs = stack_heads(v_p)
    vs_b = vs.astype(BF)
    decay_end = jnp.exp(cum_end)

    sr = lax.broadcasted_iota(jnp.int32, (2 * c, 2 * c), 0)
    sc = lax.broadcasted_iota(jnp.int32, (2 * c, 2 * c), 1)
    blk_xor = sr ^ sc
    strict, incl = sr % c > sc % c, sr % c >= sc % c
    scores = _bdot_nt(lhs, rhs)
    ab = jnp.where(strict, scores[:, :2 * c, :2 * c], 0.0)
    ak = jnp.where(strict, scores[:, :2 * c, 2 * c:], 0.0).astype(BF)
    rbk = jnp.concatenate([jnp.where(incl, scores[:, 2 * c:, :2 * c], 0.0),
                           jnp.where(incl, scores[:, 2 * c:, 2 * c:], 0.0)], axis=2).astype(BF)
    akv = _bdot(ak, vs_b)

    n = jnp.where(blk_xor == 1, ab, 0.0)
    blk = 2
    while blk < c:
        off = jnp.where((blk_xor & -blk) == blk, ab, 0.0)
        n_b = n.astype(BF)
        x = off + _bdot(off.astype(BF), n_b)
        n = n + x + _bdot(n_b, x.astype(BF))
        blk *= 2
    n_b = n.astype(BF)

    ys = []
    for gi in range(g):
        sl = slice(gi * n_pairs, (gi + 1) * n_pairs)
        s0 = s_scr[...]
        ls = _bdot_nt(lhs[sl], s0.astype(BF))
        zz = ls[:, :2 * c] + akv[sl]
        u = zz + _bdot(n_b[sl], zz.astype(BF))
        uv = jnp.concatenate([u, vs[sl]], axis=1).astype(BF)
        y = ls[:, 2 * c:] + _bdot(rbk[sl], uv)
        s_scr[...] = s0 * decay_end[sl] + _bdot_tn(uv, bk_rest[sl])
        y = y[:, :c] + y[:, c:]
        ys.append(jnp.concatenate([y[p] for p in range(n_pairs)], axis=1))
    y = jnp.concatenate(ys, axis=0) if g > 1 else ys[0]

    mean = head_sum(y) * (1.0 / HEAD_DIM)
    dlt = y - mean
    var = head_sum(dlt * dlt) * (1.0 / HEAD_DIM)
    yn = dlt * lax.rsqrt(var + A_GN_EPS) * ln_w + ln_b
    bonus = head_sum(r * k * r_k)
    o_ref[0] = ((yn + bonus * v) * gate).astype(o_ref.dtype)


def rwkv7_mix(pa, mu, wwa, g2, vec):
    b, s, cols = pa.shape
    c = WKV_CHUNK
    rows = min(WKV_GROUP * c, s)
    width = 4 * PAIR
    head = jnp.arange(PAIR, dtype=jnp.int32) // HEAD_DIM
    ones_blk = (head[:, None] == head[None, :]).astype(BF)
    return pl.pallas_call(
        functools.partial(_rwkv_kernel, c=c),
        grid=(b, s // rows),
        in_specs=[pl.BlockSpec((1, rows, cols), lambda i, j: (i, j, 0)),
                  pl.BlockSpec((1, 8, cols), lambda i, j: (i, jnp.maximum(j * (rows // 8) - 1, 0), 0)),
                  _resident((1, cols)), _resident(wwa.shape), _resident(g2.shape), _resident(vec.shape),
                  _resident(ones_blk.shape)],
        out_specs=pl.BlockSpec((1, rows, width), lambda i, j: (i, j, 0)),
        out_shape=jax.ShapeDtypeStruct((b, s, width), BF),
        scratch_shapes=[pltpu.VMEM((4, PAIR, PAIR), F32)],
        compiler_params=_params("arbitrary", "arbitrary"),
        name="rwkv7_mix",
    )(pa, pa, mu.reshape(1, cols), wwa, g2, vec, ones_blk)


def _retention_kernel(pb_ref, rope_ref, tab_ref, dm_ref, lnw_ref, o_ref, s_scr):
    ci, bi = pl.program_id(0), pl.program_id(1)
    c = pb_ref.shape[1]
    w = 4 * PAIR

    @pl.when(ci == 0)
    def _():
        s_scr[bi] = jnp.zeros(s_scr.shape[1:], F32)

    lane = lax.broadcasted_iota(jnp.int32, (c, w), 1)
    first_half = (lane % HEAD_DIM) < (HEAD_DIM // 2)
    cos, sin = rope_ref[:, :w], rope_ref[:, w:]

    def rotary(t):
        swapped = jnp.where(first_half, pltpu.roll(t, w - HEAD_DIM // 2, axis=1), pltpu.roll(t, HEAD_DIM // 2, axis=1))
        return t * cos + swapped * sin

    q = rotary(pb_ref[0, :, 0:w].astype(F32))
    k = rotary(pb_ref[0, :, w:2 * w].astype(F32)) * (HEAD_DIM ** -0.5)
    v = pb_ref[0, :, 2 * w:3 * w].astype(F32)
    g = pb_ref[0, :, 3 * w:4 * w].astype(F32)
    xi, zeta, cdecay = tab_ref[0], tab_ref[1], tab_ref[2][0:1, :]

    lo_mask, ones_blk = _pair_consts(c)
    outs = []
    for p in range(4):
        sl = slice(p * PAIR, (p + 1) * PAIR)
        q_p, k_p, v_p = q[:, sl], k[:, sl], v[:, sl]
        k_b = k_p.astype(BF)
        s_lo = _dot_nt(jnp.where(lo_mask, q_p, 0.0).astype(BF), k_b) * dm_ref[2 * p]
        s_hi = _dot_nt(jnp.where(lo_mask, 0.0, q_p).astype(BF), k_b) * dm_ref[2 * p + 1]
        vs = _stack_heads(v_p, lo_mask).astype(BF)
        inner = _dot(jnp.concatenate([s_lo, s_hi], axis=1).astype(BF), vs)
        st = s_scr[bi, p]
        cross = _dot((q_p * xi[:, sl]).astype(BF), st.astype(BF))
        ks = _stack_heads(k_p * zeta[:, sl], lo_mask).astype(BF)
        s_scr[bi, p] = st * cdecay[:, sl] + _dot_tn(ks, vs)
        o = inner + cross
        ms = _head_sum(o * o, ones_blk) * (1.0 / HEAD_DIM)
        outs.append(o * lax.rsqrt(ms + NORM_EPS))
    o = jnp.concatenate(outs, axis=1) * lnw_ref[...]
    o_ref[0] = (o * (g * jax.nn.sigmoid(g))).astype(o_ref.dtype)


def _retention_tables(s, c, n_heads):
    d = HEAD_DIM
    inv = ROPE_BASE ** (-jnp.arange(0, d, 2, dtype=F32) / d)
    ang = jnp.arange(s, dtype=F32)[:, None] * inv[None, :]
    cos, sin = jnp.cos(ang), jnp.sin(ang)
    cos_full = jnp.tile(jnp.concatenate([cos, cos], axis=1), (1, n_heads))
    sin_signed = jnp.tile(jnp.concatenate([-sin, sin], axis=1), (1, n_heads))
    rope = jnp.concatenate([cos_full, sin_signed], axis=1)
    gamma = 1.0 - 2.0 ** (-5.0 - jnp.arange(n_heads, dtype=F32))
    lg = jnp.log(gamma)[:, None]
    idx = jnp.arange(c, dtype=F32)
    rel = idx[:, None] - idx[None, :]
    dmat = jnp.where(rel >= 0, jnp.exp(jnp.maximum(rel, 0.0)[None] * lg[..., None]), 0.0)
    expand = lambda t: jnp.repeat(t.T, d, axis=1)
    zeta = expand(jnp.exp((c - 1 - idx)[None, :] * lg))
    xi = expand(jnp.exp((idx + 1)[None, :] * lg))
    cdecay = expand(jnp.broadcast_to(jnp.exp(c * lg), (n_heads, c)))
    return rope, jnp.stack([xi, zeta, cdecay]), dmat


def retention_mix(pb, ln_w):
    b, s, cols = pb.shape
    c = RET_CHUNK
    width = cols // 4
    rope, tab, dmat = _retention_tables(s, c, width // HEAD_DIM)
    return pl.pallas_call(
        _retention_kernel,
        grid=(s // c, b),
        in_specs=[pl.BlockSpec((1, c, cols), lambda j, i: (i, j, 0)),
                  pl.BlockSpec((c, 2 * width), lambda j, i: (j, 0)),
                  _resident(tab.shape), _resident(dmat.shape), _resident((1, width))],
        out_specs=pl.BlockSpec((1, c, width), lambda j, i: (i, j, 0)),
        out_shape=jax.ShapeDtypeStruct((b, s, width), BF),
        scratch_shapes=[pltpu.VMEM((b, 4, PAIR, PAIR), F32)],
        compiler_params=_params("arbitrary", "arbitrary"),
        name="retention_mix",
    )(pb, rope, tab, dmat, ln_w.reshape(1, width))


def _t5_bucket(rel):
    n = jnp.maximum(rel, 0)
    max_exact = REL_BUCKETS // 2
    large = max_exact + (jnp.log(jnp.maximum(n, max_exact).astype(F32) / max_exact)
                         / math.log(REL_MAX_DIST / max_exact) * (REL_BUCKETS - max_exact)).astype(jnp.int32)
    large = jnp.minimum(large, REL_BUCKETS - 1)
    return jnp.where(n < max_exact, n, large)


def _qkv_proj_kernel(x_ref, g_ref, wqk_ref, wvt_ref, qk_ref, vt_ref, *, n_chunk):
    xn = _rms(x_ref[...], g_ref[...]).astype(BF)
    for c in range(0, qk_ref.shape[-1], n_chunk):
        qk_ref[:, c:c + n_chunk] = _dot(xn, wqk_ref[:, c:c + n_chunk]).astype(BF)
    n_heads, n_sub, rows, t = vt_ref.shape[1:]
    for h in range(n_heads):
        vt = _dot_nt(wvt_ref[h * PAIR:(h + 1) * PAIR, :], xn)
        for j in range(n_sub):
            vt_ref[0, h, j, :PAIR, :] = vt[:, j * t:(j + 1) * t].astype(BF)
            vt_ref[0, h, j, PAIR:, :] = jnp.ones((rows - PAIR, t), BF)


def qkv_proj(x, g, w_qk, w_vt, b, tm=512, n_chunk=512):
    m, k = x.shape
    s = m // b
    tm = min(tm, s)
    t = min(ATTN_TILE, s)
    n_heads = w_vt.shape[0] // PAIR
    per_b = s // tm
    return pl.pallas_call(
        functools.partial(_qkv_proj_kernel, n_chunk=n_chunk),
        grid=(m // tm,),
        in_specs=[pl.BlockSpec((tm, k), lambda i: (i, 0)), _resident((1, k)), _resident(w_qk.shape),
                  _resident(w_vt.shape)],
        out_specs=[pl.BlockSpec((tm, w_qk.shape[1]), lambda i: (i, 0)),
                   pl.BlockSpec((1, n_heads, tm // t, VT_ROWS, t), lambda i: (i // per_b, 0, i % per_b, 0, 0))],
        out_shape=[jax.ShapeDtypeStruct((m, w_qk.shape[1]), BF),
                   jax.ShapeDtypeStruct((b, n_heads, s // t, VT_ROWS, t), BF)],
        compiler_params=_params("parallel"),
        name="qkv_proj",
    )(x, g.reshape(1, k), w_qk, w_vt)


def _diff_attn_kernel(relb_ref, lqk_ref, q_ref, k_ref, vt_ref, bkt_ref, lnw_ref, o_ref,
                      bias_scr, m_scr, acc_scr, sa_scr, sb_scr, *, lam_init):
    hi, bi, qi = pl.program_id(0), pl.program_id(1), pl.program_id(2)
    t = q_ref.shape[1]
    far_bias = relb_ref[REL_BUCKETS - 1, hi]

    @pl.when((bi == 0) & (qi == 0))
    def _():
        for tile in range(2):
            bkt = bkt_ref[tile]
            bias = jnp.full((t, t), MASK_VALUE, F32)
            for bucket in range(REL_BUCKETS):
                bias = jnp.where(bkt == bucket, relb_ref[bucket, hi] - far_bias, bias)
            bias_scr[tile] = bias

    lane = lax.broadcasted_iota(jnp.int32, (t, PAIR), 1)
    q = q_ref[0] * (HEAD_DIM ** -0.5)
    q_halves = (jnp.where(lane < HEAD_DIM, q, 0.0).astype(BF), jnp.where(lane < HEAD_DIM, 0.0, q).astype(BF))

    m_scr[...] = jnp.full(m_scr.shape, MASK_VALUE, F32)
    acc_scr[...] = jnp.zeros(acc_scr.shape, F32)

    def scores(kj, bias_tile, s_scr):
        k_t = k_ref[0, pl.ds(pl.multiple_of(kj * t, t), t), :]
        for c in range(2):
            s = _dot_nt(k_t, q_halves[c])
            s_scr[c] = s if bias_tile is None else s + bias_scr[bias_tile]

    def absorb(kj, s_scr):
        v_t = vt_ref[0, 0, kj]
        for c in range(2):
            s = s_scr[c]
            m_old = m_scr[c]
            m_new = jnp.maximum(m_old, jnp.max(s, axis=0, keepdims=True))
            alpha = jnp.exp(m_old - m_new)
            p = jnp.exp(s - m_new).astype(BF)
            acc_scr[c] = alpha * acc_scr[c] + _dot(v_t, p)
            m_scr[c] = m_new

    n_far = jnp.maximum(qi - 1, 0)
    scores(qi, 0, sa_scr)

    @pl.when(qi == 0)
    def _():
        absorb(qi, sa_scr)

    @pl.when(qi >= 1)
    def _():
        scores(qi - 1, 1, sb_scr)
        absorb(qi, sa_scr)

        def two_far(i, carry):
            scores(2 * i, None, sa_scr)
            absorb(jnp.where(i == 0, qi - 1, 2 * i - 1), sb_scr)
            scores(2 * i + 1, None, sb_scr)
            absorb(2 * i, sa_scr)
            return carry

        lax.fori_loop(0, n_far // 2, two_far, 0)

        @pl.when(n_far % 2 == 1)
        def _():
            scores(n_far - 1, None, sa_scr)
            absorb(jnp.where(n_far == 1, qi - 1, n_far - 2), sb_scr)
            absorb(n_far - 1, sa_scr)

        @pl.when(n_far % 2 == 0)
        def _():
            absorb(jnp.where(n_far == 0, qi - 1, n_far - 1), sb_scr)

    lam = (jnp.exp(jnp.sum(lqk_ref[0:1, :] * lqk_ref[1:2, :], axis=-1, keepdims=True))
           - jnp.exp(jnp.sum(lqk_ref[2:3, :] * lqk_ref[3:4, :], axis=-1, keepdims=True)) + lam_init)
    acc1, acc2 = acc_scr[0], acc_scr[1]
    o_t = (acc1[:PAIR] * (1.0 / acc1[PAIR:PAIR + 1])
           - lam * (acc2[:PAIR] * (1.0 / acc2[PAIR:PAIR + 1])))
    o_t = o_t * lax.rsqrt(jnp.mean(o_t * o_t, axis=0, keepdims=True) + NORM_EPS)
    o_ref[0] = (o_t.T * lnw_ref[...] * (1.0 - lam_init)).astype(o_ref.dtype)


def diff_attention(qk, vt, rel_bias, lqk, ln_w, lam_init):
    b, s, cols = qk.shape
    d = cols // 2
    n_heads = d // PAIR
    t = vt.shape[-1]
    pos = jnp.arange(t, dtype=jnp.int32)
    rel = pos[None, :] - pos[:, None]
    buckets = jnp.stack([jnp.where(rel >= 0, _t5_bucket(rel), -1), _t5_bucket(rel + t)])
    return pl.pallas_call(
        functools.partial(_diff_attn_kernel, lam_init=lam_init),
        grid=(n_heads, b, s // t),
        in_specs=[pl.BlockSpec(memory_space=pltpu.SMEM),
                  _resident(lqk.shape),
                  pl.BlockSpec((1, t, PAIR), lambda h, i, j: (i, j, h)),
                  pl.BlockSpec((1, s, PAIR), lambda h, i, j: (i, 0, n_heads + h)),
                  pl.BlockSpec((1, 1, s // t, VT_ROWS, t), lambda h, i, j: (i, h, 0, 0, 0)),
                  _resident(buckets.shape), _resident((1, PAIR))],
        out_specs=pl.BlockSpec((1, t, PAIR), lambda h, i, j: (i, j, h)),
        out_shape=jax.ShapeDtypeStruct((b, s, d), BF),
        scratch_shapes=[pltpu.VMEM((2, t, t), F32), pltpu.VMEM((2, 1, t), F32), pltpu.VMEM((2, VT_ROWS, t), F32),
                        pltpu.VMEM((2, t, t), F32), pltpu.VMEM((2, t, t), F32)],
        compiler_params=_params("arbitrary", "arbitrary", "arbitrary"),
        name="diff_attention",
    )(rel_bias, lqk, qk, qk, vt, buckets, ln_w.reshape(1, PAIR))


def kernel(x, mem, rel_bias, mem_norm_w, final_norm_w, norm_mix_w, norm_cross_w, norm_mlp_w, xattn_w_q, xattn_w_kv, xattn_w_o, mlp_w1, mlp_w2, hyb_w_in, rwkv_mu, rwkv_w0, rwkv_w2, rwkv_a0, rwkv_a2, rwkv_g2, rwkv_k_k, rwkv_k_a, rwkv_r_k, rwkv_ln_w, rwkv_ln_b, ret_ln_w, hyb_w_out, diff_w_in, diff_lq1, diff_lk1, diff_lq2, diff_lk2, diff_ln_w, diff_w_out):
    b, s, d = x.shape
    n_mem = mem.shape[1]
    depth = norm_mix_w.shape[0]
    a_width = rwkv_w0.shape[1]
    a_cols = rwkv_mu.shape[1]
    m = b * s
    bf = lambda t: t.astype(BF)

    h = x.reshape(m, d)
    mem2 = mem.reshape(b * n_mem, d)
    for layer in range(depth):
        i = layer // 2
        if layer % 2 == 0:
            pa, pb = norm_matmul(h, norm_mix_w[layer], bf(hyb_w_in[i]),
                                 (a_cols, hyb_w_in.shape[2] - a_cols), (F32, BF))
            lora = rwkv_w2.shape[1]
            zeros = jnp.zeros((lora, a_width), F32)
            wwa = bf(jnp.concatenate([jnp.concatenate([rwkv_w2[i], zeros], axis=1),
                                      jnp.concatenate([zeros, rwkv_a2[i]], axis=1)], axis=0))
            vec = jnp.stack([rwkv_w0[i], rwkv_a0[i], rwkv_k_k[i], rwkv_k_a[i], rwkv_r_k[i].reshape(-1),
                             rwkv_ln_w[i], rwkv_ln_b[i], jnp.zeros((a_width,), F32)])
            y_a = rwkv7_mix(pa.reshape(b, s, -1), rwkv_mu[i], wwa, bf(rwkv_g2[i]), vec)
            y_b = retention_mix(pb.reshape(b, s, -1), ret_ln_w[i])
            w_out = bf(hyb_w_out[i])
            h = matmul_residual([y_a.reshape(m, -1), y_b.reshape(m, -1)], [w_out[:a_width], w_out[a_width:]], h)
        else:
            lam_init = 0.8 - 0.6 * math.exp(-0.3 * layer)
            qk, vt = qkv_proj(h, norm_mix_w[layer], bf(diff_w_in[i][:, :2 * d]), bf(diff_w_in[i][:, 2 * d:].T), b)
            lqk = jnp.stack([diff_lq1[i], diff_lk1[i], diff_lq2[i], diff_lk2[i]])
            o = diff_attention(qk.reshape(b, s, -1), vt, rel_bias, lqk, diff_ln_w[i], lam_init)
            h = matmul_residual([o.reshape(m, -1)], [bf(diff_w_out[i])], h)
        (kv,) = norm_matmul(mem2, mem_norm_w, bf(xattn_w_kv[layer]), (2 * d,), (BF,))
        h = xattn_block(h.reshape(b, s, d), norm_cross_w[layer], bf(xattn_w_q[layer]),
                        kv.reshape(b, n_mem, 2 * d), bf(xattn_w_o[layer])).reshape(m, d)
        h = mlp_block(h, norm_mlp_w[layer], bf(mlp_w1[layer]), bf(mlp_w2[layer]), final_norm_w,
                      final_norm=(layer == depth - 1))
    return h.reshape(b, s, d)
```

```python
import functools
import math

import jax
import jax.numpy as jnp
from jax import lax
from jax.experimental import pallas as pl
from jax.experimental.pallas import tpu as pltpu

BF = jnp.bfloat16
F32 = jnp.float32

NORM_EPS = 1e-6
HEAD_DIM = 64
PAIR = 2 * HEAD_DIM
A_GN_EPS = 64e-5
ROPE_BASE = 10000.0
REL_BUCKETS = 32
REL_MAX_DIST = 128
X_HEADS = 4
WKV_CHUNK = 64
WKV_GROUP = 4
RET_CHUNK = 128
RET_GROUP = 4
ATTN_TILE = 512
VT_ROWS = PAIR + 16
MASK_VALUE = -1e30
VMEM_LIMIT_BYTES = 56 * 1024 * 1024


def _dot(a, b):
    return jnp.dot(a, b, preferred_element_type=F32)


def _dot_nt(a, b):
    return lax.dot_general(a, b, (((1,), (1,)), ((), ())), preferred_element_type=F32)


def _dot_tn(a, b):
    return lax.dot_general(a, b, (((0,), (0,)), ((), ())), preferred_element_type=F32)


def _rms(x, g):
    ms = jnp.mean(x * x, axis=-1, keepdims=True)
    return x * lax.rsqrt(ms + NORM_EPS) * g


def _split_bf16(x, parts):
    out = []
    for _ in range(parts):
        hi = x.astype(BF)
        out.append(hi)
        x = x - hi.astype(F32)
    return out


def _head_sum(x, ones_blk):
    return sum(_dot(part, ones_blk) for part in _split_bf16(x, 2))


def _pair_consts(rows):
    lane = lax.broadcasted_iota(jnp.int32, (rows, PAIR), 1)
    lo_mask = lane < HEAD_DIM
    r = lax.broadcasted_iota(jnp.int32, (PAIR, PAIR), 0)
    c = lax.broadcasted_iota(jnp.int32, (PAIR, PAIR), 1)
    ones_blk = jnp.where((r < HEAD_DIM) == (c < HEAD_DIM), 1.0, 0.0).astype(BF)
    return lo_mask, ones_blk


def _stack_heads(x, lo_mask):
    return jnp.concatenate([jnp.where(lo_mask, x, 0.0), jnp.where(lo_mask, 0.0, x)], axis=0)


def _params(*sem):
    return pltpu.CompilerParams(dimension_semantics=sem, vmem_limit_bytes=VMEM_LIMIT_BYTES)


def _resident(shape):
    nd = len(shape)
    return pl.BlockSpec(shape, lambda *_: (0,) * nd, pipeline_mode=pl.Buffered(1))


def _norm_matmul_kernel(x_ref, g_ref, w_ref, *o_refs, n_chunk):
    xn = _rms(x_ref[...], g_ref[...]).astype(BF)
    col = 0
    for o_ref in o_refs:
        n = o_ref.shape[-1]
        for c in range(0, n, n_chunk):
            cc = min(n_chunk, n - c)
            o_ref[:, c:c + cc] = _dot(xn, w_ref[:, col + c:col + c + cc]).astype(o_ref.dtype)
        col += n


def norm_matmul(x, g, w, splits, dtypes, tm=512, n_chunk=512):
    m, k = x.shape
    tm = min(tm, m)
    assert m % tm == 0 and sum(splits) == w.shape[1]
    return pl.pallas_call(
        functools.partial(_norm_matmul_kernel, n_chunk=n_chunk),
        grid=(m // tm,),
        in_specs=[pl.BlockSpec((tm, k), lambda i: (i, 0)), _resident((1, k)), _resident(w.shape)],
        out_specs=[pl.BlockSpec((tm, n), lambda i: (i, 0)) for n in splits],
        out_shape=[jax.ShapeDtypeStruct((m, n), dt) for n, dt in zip(splits, dtypes)],
        compiler_params=_params("parallel"),
        name="norm_matmul",
    )(x, g.reshape(1, k), w)


def _mlp_kernel(h_ref, g_ref, w1_ref, w2_ref, gf_ref, o_ref, a_scr, *, f_chunk, final_norm):
    h = h_ref[...]
    xn = _rms(h, g_ref[...]).astype(BF)
    for c in range(0, a_scr.shape[1], f_chunk):
        a = jnp.maximum(_dot(xn, w1_ref[:, c:c + f_chunk]), 0.0)
        a_scr[:, c:c + f_chunk] = (a * a).astype(BF)
    out = h + _dot(a_scr[...], w2_ref[...])
    if final_norm:
        out = _rms(out, gf_ref[...])
    o_ref[...] = out


def mlp_block(h, g, w1, w2, g_final, final_norm, tm=512, f_chunk=512):
    m, d = h.shape
    tm = min(tm, m)
    f = w1.shape[1]
    return pl.pallas_call(
        functools.partial(_mlp_kernel, f_chunk=f_chunk, final_norm=final_norm),
        grid=(m // tm,),
        in_specs=[pl.BlockSpec((tm, d), lambda i: (i, 0)), _resident((1, d)), _resident(w1.shape),
                  _resident(w2.shape), _resident((1, d))],
        out_specs=pl.BlockSpec((tm, d), lambda i: (i, 0)),
        out_shape=jax.ShapeDtypeStruct((m, d), F32),
        scratch_shapes=[pltpu.VMEM((tm, f), BF)],
        compiler_params=_params("parallel"),
        name="mlp_block",
    )(h, g.reshape(1, d), w1, w2, g_final.reshape(1, d))


def _xattn_kernel(*refs, n_in):
    y_refs, w_refs = refs[:n_in], refs[n_in:2 * n_in]
    h_ref, g_ref, wq_ref, kv_ref, wo_ref, o_ref, a_scr = refs[2 * n_in:]
    h = h_ref[0]
    for y_ref, w_ref in zip(y_refs, w_refs):
        h = h + _dot(y_ref[0], w_ref[...])
    d = h.shape[-1]
    hd = d // X_HEADS
    xn = _rms(h, g_ref[...]).astype(BF)
    q = (_dot(xn, wq_ref[...]) * (hd ** -0.5)).astype(BF)
    for i in range(X_HEADS):
        k = kv_ref[0, :, i * hd:(i + 1) * hd]
        v = kv_ref[0, :, d + i * hd:d + (i + 1) * hd]
        s = _dot_nt(q[:, i * hd:(i + 1) * hd], k)
        p = jnp.exp(s - jnp.max(s, axis=-1, keepdims=True))
        o = _dot(p.astype(BF), v) / jnp.sum(p, axis=-1, keepdims=True)
        a_scr[:, i * hd:(i + 1) * hd] = o.astype(BF)
    o_ref[0] = h + _dot(a_scr[...], wo_ref[...])


def xattn_block(ys, ws, h, g, wq, kv, wo, tm=512):
    b, s, d = h.shape
    tm = min(tm, s)
    n_in = len(ys)
    return pl.pallas_call(
        functools.partial(_xattn_kernel, n_in=n_in),
        grid=(b, s // tm),
        in_specs=([pl.BlockSpec((1, tm, y.shape[2]), lambda i, j: (i, j, 0)) for y in ys]
                  + [_resident(w.shape) for w in ws]
                  + [pl.BlockSpec((1, tm, d), lambda i, j: (i, j, 0)), _resident((1, d)), _resident(wq.shape),
                     pl.BlockSpec((1,) + kv.shape[1:], lambda i, j: (i, 0, 0)), _resident(wo.shape)]),
        out_specs=pl.BlockSpec((1, tm, d), lambda i, j: (i, j, 0)),
        out_shape=jax.ShapeDtypeStruct((b, s, d), F32),
        scratch_shapes=[pltpu.VMEM((tm, d), BF)],
        compiler_params=_params("parallel", "parallel"),
        name="xattn_block",
    )(*ys, *ws, h, g.reshape(1, d), wq, kv, wo)


def _bdot(a, b):
    return lax.dot_general(a, b, (((2,), (1,)), ((0,), (0,))), preferred_element_type=F32)


def _bdot_nt(a, b):
    return lax.dot_general(a, b, (((2,), (2,)), ((0,), (0,))), preferred_element_type=F32)


def _bdot_tn(a, b):
    return lax.dot_general(a, b, (((1,), (1,)), ((0,), (0,))), preferred_element_type=F32)


def _rwkv_kernel(pa_ref, prev_ref, mu_ref, wwa_ref, g2_ref, vec_ref, ones_ref, o_ref, s_scr, *, c):
    ci = pl.program_id(1)
    rows = pa_ref.shape[1]
    g = rows // c
    n_pairs = 4
    w = n_pairs * PAIR

    @pl.when(ci == 0)
    def _():
        s_scr[...] = jnp.zeros_like(s_scr)

    pa = pa_ref[0]
    prev_last = jnp.where(ci == 0, 0.0, prev_ref[0][7:8, :])
    row = lax.broadcasted_iota(jnp.int32, pa.shape, 0)
    shifted = jnp.where(row == 0, prev_last, pltpu.roll(pa, 1, axis=0))
    x = pa + (shifted - pa) * mu_ref[...]

    r, k, v = x[:, 0:w], x[:, w:2 * w], x[:, 2 * w:3 * w]
    wa, gd = x[:, 3 * w:3 * w + PAIR], x[:, 3 * w + PAIR:3 * w + 2 * PAIR]
    w0, a0, k_k, k_a, r_k, ln_w, ln_b = (vec_ref[i:i + 1, :] for i in range(7))

    lane = lax.broadcasted_iota(jnp.int32, (rows, PAIR), 1)
    ones_blk = ones_ref[...]

    def head_sum(t):
        t_b = t.astype(BF)
        return jnp.concatenate([_dot(t_b[:, p * PAIR:(p + 1) * PAIR], ones_blk) for p in range(n_pairs)], axis=1)

    z = jnp.where(lane < HEAD_DIM, jnp.tanh(wa), wa).astype(BF)
    twa = _dot(z, wwa_ref[...])
    neg = -(w0 + twa[:, :w])
    softplus = jnp.maximum(neg, 0.0) + jnp.log1p(jnp.exp(-jnp.abs(neg)))
    logw = -jnp.exp(-softplus - 0.5)
    a = jax.nn.sigmoid(a0 + twa[:, w:])
    gate = _dot(jax.nn.sigmoid(gd).astype(BF), g2_ref[...])

    kk = k * k_k
    kk = kk / jnp.maximum(jnp.sqrt(head_sum(kk * kk)), 1e-12)
    k = k * (1.0 + (a - 1.0) * k_a)
    aa = -kk
    bb = kk * a

    tr = lax.broadcasted_iota(jnp.int32, (rows, rows), 0)
    tc = lax.broadcasted_iota(jnp.int32, (rows, rows), 1)
    tril = jnp.where((tr >= tc) & (tr // c == tc // c), 1.0, 0.0).astype(BF)
    cum = sum(_dot(tril, part) for part in _split_bf16(logw, 2))

    def to_pairs(t):
        return jnp.stack([t[gi * c:(gi + 1) * c, p * PAIR:(p + 1) * PAIR] for gi in range(g) for p in range(n_pairs)])

    def stack_heads(t):
        lo = lax.broadcasted_iota(jnp.int32, t.shape, 2) < HEAD_DIM
        return jnp.concatenate([jnp.where(lo, t, 0.0), jnp.where(lo, 0.0, t)], axis=1)

    cum_p, logw_p, r_p, k_p, v_p, a_p, b_p = (to_pairs(t) for t in (cum, logw, r, k, v, aa, bb))
    e_in = jnp.exp(cum_p)
    e_out = jnp.exp(-cum_p)
    e_prev = jnp.exp(cum_p - logw_p)
    cum_end = cum_p[:, c - 1:c, :]
    e_rest = jnp.exp(cum_end - cum_p)
    lhs = jnp.concatenate([stack_heads(a_p * e_prev), stack_heads(r_p * e_in)], axis=1).astype(BF)
    rhs = jnp.concatenate([stack_heads(b_p * e_out), stack_heads(k_p * e_out)], axis=1).astype(BF)
    bk_rest = jnp.concatenate([stack_heads(b_p * e_rest), stack_heads(k_p * e_rest)], axis=1).astype(BF)
    vs = stack_heads(v_p)
    vs_b = vs.astype(BF)
    decay_end = jnp.exp(cum_end)

    sr = lax.broadcasted_iota(jnp.int32, (2 * c, 2 * c), 0)
    sc = lax.broadcasted_iota(jnp.int32, (2 * c, 2 * c), 1)
    blk_xor = sr ^ sc
    strict, incl = sr % c > sc % c, sr % c >= sc % c
    scores = _bdot_nt(lhs, rhs)
    ab = jnp.where(strict, scores[:, :2 * c, :2 * c], 0.0)
    ak = jnp.where(strict, scores[:, :2 * c, 2 * c:], 0.0).astype(BF)
    rbk = jnp.concatenate([jnp.where(incl, scores[:, 2 * c:, :2 * c], 0.0),
                           jnp.where(incl, scores[:, 2 * c:, 2 * c:], 0.0)], axis=2).astype(BF)
    akv = _bdot(ak, vs_b)

    n = jnp.where(blk_xor == 1, ab, 0.0)
    blk = 2
    while blk < c:
        off = jnp.where((blk_xor & -blk) == blk, ab, 0.0)
        n_b = n.astype(BF)
        x = off + _bdot(off.astype(BF), n_b)
        n = n + x + _bdot(n_b, x.astype(BF))
        blk *= 2
    n_b = n.astype(BF)

    ys = []
    for gi in range(g):
        sl = slice(gi * n_pairs, (gi + 1) * n_pairs)
        s0 = s_scr[...]
        ls = _bdot_nt(lhs[sl], s0.astype(BF))
        zz = ls[:, :2 * c] + akv[sl]
        u = zz + _bdot(n_b[sl], zz.astype(BF))
        uv = jnp.concatenate([u, vs[sl]], axis=1).astype(BF)
        y = ls[:, 2 * c:] + _bdot(rbk[sl], uv)
        s_scr[...] = s0 * decay_end[sl] + _bdot_tn(uv, bk_rest[sl])
        y = y[:, :c] + y[:, c:]
        ys.append(jnp.concatenate([y[p] for p in range(n_pairs)], axis=1))
    y = jnp.concatenate(ys, axis=0) if g > 1 else ys[0]

    mean = head_sum(y) * (1.0 / HEAD_DIM)
    dlt = y - mean
    var = head_sum(dlt * dlt) * (1.0 / HEAD_DIM)
    yn = dlt * lax.rsqrt(var + A_GN_EPS) * ln_w + ln_b
    bonus = head_sum(r * k * r_k)
    o_ref[0] = ((yn + bonus * v) * gate).astype(o_ref.dtype)


def rwkv7_mix(pa, mu, wwa, g2, vec):
    b, s, cols = pa.shape
    c = WKV_CHUNK
    rows = min(WKV_GROUP * c, s)
    width = 4 * PAIR
    head = jnp.arange(PAIR, dtype=jnp.int32) // HEAD_DIM
    ones_blk = (head[:, None] == head[None, :]).astype(BF)
    return pl.pallas_call(
        functools.partial(_rwkv_kernel, c=c),
        grid=(b, s // rows),
        in_specs=[pl.BlockSpec((1, rows, cols), lambda i, j: (i, j, 0)),
                  pl.BlockSpec((1, 8, cols), lambda i, j: (i, jnp.maximum(j * (rows // 8) - 1, 0), 0)),
                  _resident((1, cols)), _resident(wwa.shape), _resident(g2.shape), _resident(vec.shape),
                  _resident(ones_blk.shape)],
        out_specs=pl.BlockSpec((1, rows, width), lambda i, j: (i, j, 0)),
        out_shape=jax.ShapeDtypeStruct((b, s, width), BF),
        scratch_shapes=[pltpu.VMEM((4, PAIR, PAIR), F32)],
        compiler_params=_params("arbitrary", "arbitrary"),
        name="rwkv7_mix",
    )(pa, pa, mu.reshape(1, cols), wwa, g2, vec, ones_blk)


def _retention_kernel(pb_ref, rope_ref, tab_ref, dm_ref, lnw_ref, o_ref, s_scr, *, c):
    ci, bi = pl.program_id(0), pl.program_id(1)
    rows = pb_ref.shape[1]
    g = rows // c
    n_pairs = 4
    w = n_pairs * PAIR

    @pl.when(ci == 0)
    def _():
        s_scr[bi] = jnp.zeros(s_scr.shape[1:], F32)

    lane = lax.broadcasted_iota(jnp.int32, (rows, w), 1)
    first_half = (lane % HEAD_DIM) < (HEAD_DIM // 2)
    cos, sin = rope_ref[:, :w], rope_ref[:, w:]

    def rotary(t):
        swapped = jnp.where(first_half, pltpu.roll(t, w - HEAD_DIM // 2, axis=1), pltpu.roll(t, HEAD_DIM // 2, axis=1))
        return t * cos + swapped * sin

    def to_pairs(t):
        return jnp.stack([t[gi * c:(gi + 1) * c, p * PAIR:(p + 1) * PAIR] for gi in range(g) for p in range(n_pairs)])

    def stack_heads(t):
        lo = lax.broadcasted_iota(jnp.int32, t.shape, 2) < HEAD_DIM
        return jnp.concatenate([jnp.where(lo, t, 0.0), jnp.where(lo, 0.0, t)], axis=1)

    q = rotary(pb_ref[0, :, 0:w].astype(F32))
    k = rotary(pb_ref[0, :, w:2 * w].astype(F32)) * (HEAD_DIM ** -0.5)
    gate = pb_ref[0, :, 3 * w:4 * w].astype(F32)
    xi, zeta, cdecay = tab_ref[0], tab_ref[1], tab_ref[2][0:1, :]

    q_p, k_p, qx_p, kz_p = (to_pairs(t) for t in (q, k, q * xi, k * zeta))
    v_p = to_pairs(pb_ref[0, :, 2 * w:3 * w].astype(F32))
    lo = lax.broadcasted_iota(jnp.int32, q_p.shape, 2) < HEAD_DIM
    k_b = k_p.astype(BF)
    dm_lo = jnp.stack([dm_ref[2 * p] for _ in range(g) for p in range(n_pairs)])
    dm_hi = jnp.stack([dm_ref[2 * p + 1] for _ in range(g) for p in range(n_pairs)])
    s_lo = _bdot_nt(jnp.where(lo, q_p, 0.0).astype(BF), k_b) * dm_lo
    s_hi = _bdot_nt(jnp.where(lo, 0.0, q_p).astype(BF), k_b) * dm_hi
    vs = stack_heads(v_p).astype(BF)
    inner = _bdot(jnp.concatenate([s_lo, s_hi], axis=2).astype(BF), vs)
    upd = _bdot_tn(stack_heads(kz_p).astype(BF), vs)

    cd = jnp.stack([cdecay[:, p * PAIR:(p + 1) * PAIR] for p in range(n_pairs)])
    st = s_scr[bi]
    states = []
    for gi in range(g):
        states.append(st)
        st = st * cd + upd[gi * n_pairs:(gi + 1) * n_pairs]
    s_scr[bi] = st
    states = jnp.concatenate(states, axis=0) if g > 1 else states[0]
    o = inner + _bdot(qx_p.astype(BF), states.astype(BF))

    o = jnp.concatenate([jnp.concatenate([o[gi * n_pairs + p] for p in range(n_pairs)], axis=1)
                         for gi in range(g)], axis=0)
    ones_blk = _pair_consts(rows)[1]
    ms = jnp.concatenate([_head_sum((o * o)[:, p * PAIR:(p + 1) * PAIR], ones_blk) for p in range(n_pairs)], axis=1)
    o = o * lax.rsqrt(ms * (1.0 / HEAD_DIM) + NORM_EPS) * lnw_ref[...]
    o_ref[0] = (o * (gate * jax.nn.sigmoid(gate))).astype(o_ref.dtype)


def _retention_tables(s, c, n_heads):
    d = HEAD_DIM
    inv = ROPE_BASE ** (-jnp.arange(0, d, 2, dtype=F32) / d)
    ang = jnp.arange(s, dtype=F32)[:, None] * inv[None, :]
    cos, sin = jnp.cos(ang), jnp.sin(ang)
    cos_full = jnp.tile(jnp.concatenate([cos, cos], axis=1), (1, n_heads))
    sin_signed = jnp.tile(jnp.concatenate([-sin, sin], axis=1), (1, n_heads))
    rope = jnp.concatenate([cos_full, sin_signed], axis=1)
    gamma = 1.0 - 2.0 ** (-5.0 - jnp.arange(n_heads, dtype=F32))
    lg = jnp.log(gamma)[:, None]
    idx = jnp.arange(c, dtype=F32)
    rel = idx[:, None] - idx[None, :]
    dmat = jnp.where(rel >= 0, jnp.exp(jnp.maximum(rel, 0.0)[None] * lg[..., None]), 0.0)
    expand = lambda t: jnp.repeat(t.T, d, axis=1)
    zeta = expand(jnp.exp((c - 1 - idx)[None, :] * lg))
    xi = expand(jnp.exp((idx + 1)[None, :] * lg))
    cdecay = expand(jnp.broadcast_to(jnp.exp(c * lg), (n_heads, c)))
    return rope, jnp.stack([xi, zeta, cdecay]), dmat


def retention_mix(pb, ln_w):
    b, s, cols = pb.shape
    c = RET_CHUNK
    rows = min(RET_GROUP * c, s)
    width = cols // 4
    rope, tab, dmat = _retention_tables(s, c, width // HEAD_DIM)
    tab = jnp.tile(tab, (1, rows // c, 1))
    return pl.pallas_call(
        functools.partial(_retention_kernel, c=c),
        grid=(s // rows, b),
        in_specs=[pl.BlockSpec((1, rows, cols), lambda j, i: (i, j, 0)),
                  pl.BlockSpec((rows, 2 * width), lambda j, i: (j, 0)),
                  _resident(tab.shape), _resident(dmat.shape), _resident((1, width))],
        out_specs=pl.BlockSpec((1, rows, width), lambda j, i: (i, j, 0)),
        out_shape=jax.ShapeDtypeStruct((b, s, width), BF),
        scratch_shapes=[pltpu.VMEM((b, 4, PAIR, PAIR), F32)],
        compiler_params=_params("arbitrary", "arbitrary"),
        name="retention_mix",
    )(pb, rope, tab, dmat, ln_w.reshape(1, width))


def _t5_bucket(rel):
    n = jnp.maximum(rel, 0)
    max_exact = REL_BUCKETS // 2
    large = max_exact + (jnp.log(jnp.maximum(n, max_exact).astype(F32) / max_exact)
                         / math.log(REL_MAX_DIST / max_exact) * (REL_BUCKETS - max_exact)).astype(jnp.int32)
    large = jnp.minimum(large, REL_BUCKETS - 1)
    return jnp.where(n < max_exact, n, large)


def _qkv_proj_kernel(x_ref, g_ref, wqk_ref, wvt_ref, qk_ref, vt_ref, *, n_chunk):
    xn = _rms(x_ref[...], g_ref[...]).astype(BF)
    for c in range(0, qk_ref.shape[-1], n_chunk):
        qk_ref[:, c:c + n_chunk] = _dot(xn, wqk_ref[:, c:c + n_chunk]).astype(BF)
    n_heads, n_sub, rows, t = vt_ref.shape[1:]
    vt = _dot_nt(wvt_ref[...], xn)
    for h in range(n_heads):
        for j in range(n_sub):
            vt_ref[0, h, j, :PAIR, :] = vt[h * PAIR:(h + 1) * PAIR, j * t:(j + 1) * t].astype(BF)
            vt_ref[0, h, j, PAIR:, :] = jnp.ones((rows - PAIR, t), BF)


def qkv_proj(x, g, w_qk, w_vt, b, tm=512, n_chunk=512):
    m, k = x.shape
    s = m // b
    tm = min(tm, s)
    t = min(ATTN_TILE, s)
    n_heads = w_vt.shape[0] // PAIR
    per_b = s // tm
    return pl.pallas_call(
        functools.partial(_qkv_proj_kernel, n_chunk=n_chunk),
        grid=(m // tm,),
        in_specs=[pl.BlockSpec((tm, k), lambda i: (i, 0)), _resident((1, k)), _resident(w_qk.shape),
                  _resident(w_vt.shape)],
        out_specs=[pl.BlockSpec((tm, w_qk.shape[1]), lambda i: (i, 0)),
                   pl.BlockSpec((1, n_heads, tm // t, VT_ROWS, t), lambda i: (i // per_b, 0, i % per_b, 0, 0))],
        out_shape=[jax.ShapeDtypeStruct((m, w_qk.shape[1]), BF),
                   jax.ShapeDtypeStruct((b, n_heads, s // t, VT_ROWS, t), BF)],
        compiler_params=_params("parallel"),
        name="qkv_proj",
    )(x, g.reshape(1, k), w_qk, w_vt)


def _attn_schedule(n_q):
    near = [(qi, qi, qi, 0) for qi in range(n_q)] + [(qi - 1, qi, qi, 1) for qi in range(1, n_q)]
    far = [(kj, qi, qi, 2) for qi in range(n_q) for kj in range(qi - 1)]
    idle = (0, 0, n_q, 2)
    entries = [idle] + near + far + [idle]
    n_steps = len(entries) - 1
    if n_steps % 2:
        entries.append(idle)
        n_steps += 1
    n_bias_steps = len(near) + (len(near) % 2)
    return entries, n_steps, n_bias_steps


def _diff_attn_kernel(sched_ref, relb_ref, lqk_ref, q_ref, k_ref, vt_ref, bkt_ref, lnw_ref, o_ref,
                      bias_scr, m_scr, acc_scr, s_scr, *, lam_init, n_steps, n_bias_steps):
    hi, bi = pl.program_id(0), pl.program_id(1)
    t = bias_scr.shape[1]
    n_q = q_ref.shape[1] // t
    far_bias = relb_ref[REL_BUCKETS - 1, hi]

    @pl.when(bi == 0)
    def _():
        for tile in range(2):
            bkt = bkt_ref[tile]
            bias = jnp.full((t, t), MASK_VALUE, F32)
            for bucket in range(REL_BUCKETS):
                bias = jnp.where(bkt == bucket, relb_ref[bucket, hi] - far_bias, bias)
            bias_scr[tile] = bias
        bias_scr[2] = jnp.zeros((t, t), F32)

    m_scr[...] = jnp.full(m_scr.shape, MASK_VALUE, F32)
    acc_scr[...] = jnp.zeros(acc_scr.shape, F32)
    s_scr[1] = jnp.zeros(s_scr.shape[1:], F32)
    lane = lax.broadcasted_iota(jnp.int32, (t, PAIR), 1)

    def step(i, slot, with_bias):
        kj, qt, bidx = sched_ref[0, i + 1], sched_ref[1, i + 1], sched_ref[3, i + 1]
        k_t = k_ref[0, pl.ds(pl.multiple_of(kj * t, t), t), :]
        q = q_ref[0, pl.ds(pl.multiple_of(qt * t, t), t), :] * (HEAD_DIM ** -0.5)
        for c in range(2):
            q_c = jnp.where(lane < HEAD_DIM, q, 0.0) if c == 0 else jnp.where(lane < HEAD_DIM, 0.0, q)
            s = _dot_nt(k_t, q_c.astype(BF))
            s_scr[slot, c] = s + bias_scr[bidx] if with_bias else s
        st, v_t = sched_ref[2, i], vt_ref[0, 0, sched_ref[0, i]]
        for c in range(2):
            s = s_scr[1 - slot, c]
            m_old = m_scr[st, c]
            m_new = jnp.maximum(m_old, jnp.max(s, axis=0, keepdims=True))
            alpha = jnp.exp(m_old - m_new)
            p = jnp.exp(s - m_new).astype(BF)
            acc_scr[st, c] = alpha * acc_scr[st, c] + _dot(v_t, p)
            m_scr[st, c] = m_new

    def two_steps(with_bias):
        def body(j, carry):
            step(2 * j, 0, with_bias)
            step(2 * j + 1, 1, with_bias)
            return carry
        return body

    lax.fori_loop(0, n_bias_steps // 2, two_steps(True), 0)
    lax.fori_loop(n_bias_steps // 2, n_steps // 2, two_steps(False), 0)

    lam = (jnp.exp(jnp.sum(lqk_ref[0:1, :] * lqk_ref[1:2, :], axis=-1, keepdims=True))
           - jnp.exp(jnp.sum(lqk_ref[2:3, :] * lqk_ref[3:4, :], axis=-1, keepdims=True)) + lam_init)
    for qi in range(n_q):
        acc1, acc2 = acc_scr[qi, 0], acc_scr[qi, 1]
        o_t = (acc1[:PAIR] * (1.0 / acc1[PAIR:PAIR + 1])
               - lam * (acc2[:PAIR] * (1.0 / acc2[PAIR:PAIR + 1])))
        o_t = o_t * lax.rsqrt(jnp.mean(o_t * o_t, axis=0, keepdims=True) + NORM_EPS)
        o_ref[0, qi * t:(qi + 1) * t, :] = (o_t.T * lnw_ref[...] * (1.0 - lam_init)).astype(o_ref.dtype)


def diff_attention(qk, vt, rel_bias, lqk, ln_w, lam_init):
    b, s, cols = qk.shape
    d = cols // 2
    n_heads = d // PAIR
    t = vt.shape[-1]
    n_q = s // t
    pos = jnp.arange(t, dtype=jnp.int32)
    rel = pos[None, :] - pos[:, None]
    buckets = jnp.stack([jnp.where(rel >= 0, _t5_bucket(rel), -1), _t5_bucket(rel + t)])
    entries, n_steps, n_bias_steps = _attn_schedule(n_q)
    sched = jnp.asarray(entries, dtype=jnp.int32).T
    return pl.pallas_call(
        functools.partial(_diff_attn_kernel, lam_init=lam_init, n_steps=n_steps, n_bias_steps=n_bias_steps),
        grid=(n_heads, b),
        in_specs=[pl.BlockSpec(memory_space=pltpu.SMEM),
                  pl.BlockSpec(memory_space=pltpu.SMEM),
                  _resident(lqk.shape),
                  pl.BlockSpec((1, s, PAIR), lambda h, i: (i, 0, h)),
                  pl.BlockSpec((1, s, PAIR), lambda h, i: (i, 0, n_heads + h)),
                  pl.BlockSpec((1, 1, n_q, VT_ROWS, t), lambda h, i: (i, h, 0, 0, 0)),
                  _resident(buckets.shape), _resident((1, PAIR))],
        out_specs=pl.BlockSpec((1, s, PAIR), lambda h, i: (i, 0, h)),
        out_shape=jax.ShapeDtypeStruct((b, s, d), BF),
        scratch_shapes=[pltpu.VMEM((3, t, t), F32),
                        pltpu.VMEM((n_q + 1, 2, 1, t), F32),
                        pltpu.VMEM((n_q + 1, 2, VT_ROWS, t), F32),
                        pltpu.VMEM((2, 2, t, t), F32)],
        compiler_params=_params("arbitrary", "arbitrary"),
        name="diff_attention",
    )(sched, rel_bias, lqk, qk, qk, vt, buckets, ln_w.reshape(1, PAIR))


def kernel(x, mem, rel_bias, mem_norm_w, final_norm_w, norm_mix_w, norm_cross_w, norm_mlp_w, xattn_w_q, xattn_w_kv, xattn_w_o, mlp_w1, mlp_w2, hyb_w_in, rwkv_mu, rwkv_w0, rwkv_w2, rwkv_a0, rwkv_a2, rwkv_g2, rwkv_k_k, rwkv_k_a, rwkv_r_k, rwkv_ln_w, rwkv_ln_b, ret_ln_w, hyb_w_out, diff_w_in, diff_lq1, diff_lk1, diff_lq2, diff_lk2, diff_ln_w, diff_w_out):
    b, s, d = x.shape
    n_mem = mem.shape[1]
    depth = norm_mix_w.shape[0]
    a_width = rwkv_w0.shape[1]
    a_cols = rwkv_mu.shape[1]
    m = b * s
    bf = lambda t: t.astype(BF)

    h = x.reshape(m, d)
    mem2 = mem.reshape(b * n_mem, d)
    for layer in range(depth):
        i = layer // 2
        if layer % 2 == 0:
            pa, pb = norm_matmul(h, norm_mix_w[layer], bf(hyb_w_in[i]),
                                 (a_cols, hyb_w_in.shape[2] - a_cols), (F32, BF))
            lora = rwkv_w2.shape[1]
            zeros = jnp.zeros((lora, a_width), F32)
            wwa = bf(jnp.concatenate([jnp.concatenate([rwkv_w2[i], zeros], axis=1),
                                      jnp.concatenate([zeros, rwkv_a2[i]], axis=1)], axis=0))
            vec = jnp.stack([rwkv_w0[i], rwkv_a0[i], rwkv_k_k[i], rwkv_k_a[i], rwkv_r_k[i].reshape(-1),
                             rwkv_ln_w[i], rwkv_ln_b[i], jnp.zeros((a_width,), F32)])
            y_a = rwkv7_mix(pa.reshape(b, s, -1), rwkv_mu[i], wwa, bf(rwkv_g2[i]), vec)
            y_b = retention_mix(pb.reshape(b, s, -1), ret_ln_w[i])
            w_out = bf(hyb_w_out[i])
            ys, ws = [y_a, y_b], [w_out[:a_width], w_out[a_width:]]
        else:
            lam_init = 0.8 - 0.6 * math.exp(-0.3 * layer)
            qk, vt = qkv_proj(h, norm_mix_w[layer], bf(diff_w_in[i][:, :2 * d]), bf(diff_w_in[i][:, 2 * d:].T), b)
            lqk = jnp.stack([diff_lq1[i], diff_lk1[i], diff_lq2[i], diff_lk2[i]])
            o = diff_attention(qk.reshape(b, s, -1), vt, rel_bias, lqk, diff_ln_w[i], lam_init)
            ys, ws = [o], [bf(diff_w_out[i])]
        (kv,) = norm_matmul(mem2, mem_norm_w, bf(xattn_w_kv[layer]), (2 * d,), (BF,))
        h = xattn_block(ys, ws, h.reshape(b, s, d), norm_cross_w[layer], bf(xattn_w_q[layer]),
                        kv.reshape(b, n_mem, 2 * d), bf(xattn_w_o[layer])).reshape(m, d)
        h = mlp_block(h, norm_mlp_w[layer], bf(mlp_w1[layer]), bf(mlp_w2[layer]), final_norm_w,
                      final_norm=(layer == depth - 1))
    return h.reshape(b, s, d)
```

```python
import functools
import math

import jax
import jax.numpy as jnp
from jax import lax
from jax.experimental import pallas as pl
from jax.experimental.pallas import tpu as pltpu

BF = jnp.bfloat16
F32 = jnp.float32

NORM_EPS = 1e-6
HEAD_DIM = 64
PAIR = 2 * HEAD_DIM
F32_SUBLANES = 8
A_GN_EPS = 64e-5
ROPE_BASE = 10000.0
REL_BUCKETS = 32
REL_MAX_DIST = 128
X_HEADS = 4
WKV_CHUNK = 64
WKV_GROUP = 4
RET_CHUNK = 128
RET_GROUP = 4
ATTN_TILE = 512
VT_ROWS = PAIR + 16
MASK_VALUE = -1e30
LOG2_E = math.log2(math.e)
ATTN_Q_SCALE = HEAD_DIM ** -0.5 * LOG2_E
VMEM_LIMIT_BYTES = 56 * 1024 * 1024


def _dot(a, b):
    return jnp.dot(a, b, preferred_element_type=F32)


def _dot_nt(a, b):
    return lax.dot_general(a, b, (((1,), (1,)), ((), ())), preferred_element_type=F32)


def _dot_tn(a, b):
    return lax.dot_general(a, b, (((0,), (0,)), ((), ())), preferred_element_type=F32)


def _rms(x, g):
    ms = jnp.mean(x * x, axis=-1, keepdims=True)
    return x * lax.rsqrt(ms + NORM_EPS) * g


def _split_bf16(x, parts):
    out = []
    for _ in range(parts):
        hi = x.astype(BF)
        out.append(hi)
        x = x - hi.astype(F32)
    return out


def _head_sum(x, ones_blk):
    return sum(_dot(part, ones_blk) for part in _split_bf16(x, 2))


def _pair_consts(rows):
    lane = lax.broadcasted_iota(jnp.int32, (rows, PAIR), 1)
    lo_mask = lane < HEAD_DIM
    r = lax.broadcasted_iota(jnp.int32, (PAIR, PAIR), 0)
    c = lax.broadcasted_iota(jnp.int32, (PAIR, PAIR), 1)
    ones_blk = jnp.where((r < HEAD_DIM) == (c < HEAD_DIM), 1.0, 0.0).astype(BF)
    return lo_mask, ones_blk


def _stack_heads(x, lo_mask):
    return jnp.concatenate([jnp.where(lo_mask, x, 0.0), jnp.where(lo_mask, 0.0, x)], axis=0)


def _params(*sem):
    return pltpu.CompilerParams(dimension_semantics=sem, vmem_limit_bytes=VMEM_LIMIT_BYTES)


def _resident(shape):
    nd = len(shape)
    return pl.BlockSpec(shape, lambda *_: (0,) * nd, pipeline_mode=pl.Buffered(1))


def _norm_matmul_kernel(x_ref, g_ref, w_ref, *o_refs, n_chunk):
    xn = _rms(x_ref[...], g_ref[...]).astype(BF)
    col = 0
    for o_ref in o_refs:
        n = o_ref.shape[-1]
        for c in range(0, n, n_chunk):
            cc = min(n_chunk, n - c)
            o_ref[:, c:c + cc] = _dot(xn, w_ref[:, col + c:col + c + cc]).astype(o_ref.dtype)
        col += n


def norm_matmul(x, g, w, splits, dtypes, tm=512, n_chunk=512):
    m, k = x.shape
    tm = min(tm, m)
    assert m % tm == 0 and sum(splits) == w.shape[1]
    return pl.pallas_call(
        functools.partial(_norm_matmul_kernel, n_chunk=n_chunk),
        grid=(m // tm,),
        in_specs=[pl.BlockSpec((tm, k), lambda i: (i, 0)), _resident((1, k)), _resident(w.shape)],
        out_specs=[pl.BlockSpec((tm, n), lambda i: (i, 0)) for n in splits],
        out_shape=[jax.ShapeDtypeStruct((m, n), dt) for n, dt in zip(splits, dtypes)],
        compiler_params=_params("parallel"),
        name="norm_matmul",
    )(x, g.reshape(1, k), w)


def _mlp_kernel(h_ref, g_ref, w1_ref, w2_ref, gf_ref, o_ref, a_scr, *, f_chunk, final_norm):
    h = h_ref[...]
    xn = _rms(h, g_ref[...]).astype(BF)
    for c in range(0, a_scr.shape[1], f_chunk):
        a = jnp.maximum(_dot(xn, w1_ref[:, c:c + f_chunk]), 0.0)
        a_scr[:, c:c + f_chunk] = (a * a).astype(BF)
    out = h + _dot(a_scr[...], w2_ref[...])
    if final_norm:
        out = _rms(out, gf_ref[...])
    o_ref[...] = out


def mlp_block(h, g, w1, w2, g_final, final_norm, tm=512, f_chunk=512):
    m, d = h.shape
    tm = min(tm, m)
    f = w1.shape[1]
    return pl.pallas_call(
        functools.partial(_mlp_kernel, f_chunk=f_chunk, final_norm=final_norm),
        grid=(m // tm,),
        in_specs=[pl.BlockSpec((tm, d), lambda i: (i, 0)), _resident((1, d)), _resident(w1.shape),
                  _resident(w2.shape), _resident((1, d))],
        out_specs=pl.BlockSpec((tm, d), lambda i: (i, 0)),
        out_shape=jax.ShapeDtypeStruct((m, d), F32),
        scratch_shapes=[pltpu.VMEM((tm, f), BF)],
        compiler_params=_params("parallel"),
        name="mlp_block",
    )(h, g.reshape(1, d), w1, w2, g_final.reshape(1, d))


def _xattn_kernel(*refs, n_in):
    y_refs, w_refs = refs[:n_in], refs[n_in:2 * n_in]
    h_ref, g_ref, wq_ref, kv_ref, wo_ref, o_ref, a_scr = refs[2 * n_in:]
    h = h_ref[0]
    for y_ref, w_ref in zip(y_refs, w_refs):
        h = h + _dot(y_ref[0], w_ref[...])
    d = h.shape[-1]
    hd = d // X_HEADS
    xn = _rms(h, g_ref[...]).astype(BF)
    q = (_dot(xn, wq_ref[...]) * (hd ** -0.5)).astype(BF)
    for i in range(X_HEADS):
        k = kv_ref[0, :, i * hd:(i + 1) * hd]
        v = kv_ref[0, :, d + i * hd:d + (i + 1) * hd]
        s = _dot_nt(q[:, i * hd:(i + 1) * hd], k)
        p = jnp.exp(s - jnp.max(s, axis=-1, keepdims=True))
        o = _dot(p.astype(BF), v) / jnp.sum(p, axis=-1, keepdims=True)
        a_scr[:, i * hd:(i + 1) * hd] = o.astype(BF)
    o_ref[0] = h + _dot(a_scr[...], wo_ref[...])


def xattn_block(ys, ws, h, g, wq, kv, wo, tm=512):
    b, s, d = h.shape
    tm = min(tm, s)
    n_in = len(ys)
    return pl.pallas_call(
        functools.partial(_xattn_kernel, n_in=n_in),
        grid=(b, s // tm),
        in_specs=([pl.BlockSpec((1, tm, y.shape[2]), lambda i, j: (i, j, 0)) for y in ys]
                  + [_resident(w.shape) for w in ws]
                  + [pl.BlockSpec((1, tm, d), lambda i, j: (i, j, 0)), _resident((1, d)), _resident(wq.shape),
                     pl.BlockSpec((1,) + kv.shape[1:], lambda i, j: (i, 0, 0)), _resident(wo.shape)]),
        out_specs=pl.BlockSpec((1, tm, d), lambda i, j: (i, j, 0)),
        out_shape=jax.ShapeDtypeStruct((b, s, d), F32),
        scratch_shapes=[pltpu.VMEM((tm, d), BF)],
        compiler_params=_params("parallel", "parallel"),
        name="xattn_block",
    )(*ys, *ws, h, g.reshape(1, d), wq, kv, wo)


def _bdot(a, b):
    return lax.dot_general(a, b, (((2,), (1,)), ((0,), (0,))), preferred_element_type=F32)


def _bdot_nt(a, b):
    return lax.dot_general(a, b, (((2,), (2,)), ((0,), (0,))), preferred_element_type=F32)


def _bdot_tn(a, b):
    return lax.dot_general(a, b, (((1,), (1,)), ((0,), (0,))), preferred_element_type=F32)


def _rwkv_kernel(pa_ref, prev_ref, mu_ref, wwa_ref, g2_ref, vec_ref, ones_ref, o_ref, s_scr, *, c):
    ci = pl.program_id(1)
    rows = pa_ref.shape[1]
    g = rows // c
    n_pairs = 4
    w = n_pairs * PAIR

    @pl.when(ci == 0)
    def _():
        s_scr[...] = jnp.zeros_like(s_scr)

    pa = pa_ref[0]
    prev_last = jnp.where(ci == 0, 0.0, prev_ref[0][7:8, :])
    row = lax.broadcasted_iota(jnp.int32, pa.shape, 0)
    shifted = jnp.where(row == 0, prev_last, pltpu.roll(pa, 1, axis=0))
    x = pa + (shifted - pa) * mu_ref[...]

    r, k, v = x[:, 0:w], x[:, w:2 * w], x[:, 2 * w:3 * w]
    wa, gd = x[:, 3 * w:3 * w + PAIR], x[:, 3 * w + PAIR:3 * w + 2 * PAIR]
    w0, a0, k_k, k_a, r_k, ln_w, ln_b = (vec_ref[i:i + 1, :] for i in range(7))

    lane = lax.broadcasted_iota(jnp.int32, (rows, PAIR), 1)
    ones_blk = ones_ref[...]

    def head_sum(t):
        t_b = t.astype(BF)
        return jnp.concatenate([_dot(t_b[:, p * PAIR:(p + 1) * PAIR], ones_blk) for p in range(n_pairs)], axis=1)

    z = jnp.where(lane < HEAD_DIM, jnp.tanh(wa), wa).astype(BF)
    twa = _dot(z, wwa_ref[...])
    neg = -(w0 + twa[:, :w])
    softplus = jnp.maximum(neg, 0.0) + jnp.log(1.0 + jnp.exp(-jnp.abs(neg)))
    logw = -jnp.exp(-softplus - 0.5)
    a = jax.nn.sigmoid(a0 + twa[:, w:])
    gate = _dot(jax.nn.sigmoid(gd).astype(BF), g2_ref[...])

    kk = k * k_k
    kk = kk * lax.rsqrt(jnp.maximum(head_sum(kk * kk), 1e-24))
    k = k * (1.0 + (a - 1.0) * k_a)
    aa = -kk
    bb = kk * a

    tr = lax.broadcasted_iota(jnp.int32, (rows, rows), 0)
    tc = lax.broadcasted_iota(jnp.int32, (rows, rows), 1)
    tril = jnp.where((tr >= tc) & (tr // c == tc // c), 1.0, 0.0).astype(BF)
    cum = sum(_dot(tril, part) for part in _split_bf16(logw, 2))

    def stack_heads(t):
        lo = lax.broadcasted_iota(jnp.int32, t.shape, 2) < HEAD_DIM
        return jnp.concatenate([jnp.where(lo, t, 0.0), jnp.where(lo, 0.0, t)], axis=1)

    sr = lax.broadcasted_iota(jnp.int32, (2 * c, 2 * c), 0)
    sc = lax.broadcasted_iota(jnp.int32, (2 * c, 2 * c), 1)
    blk_xor = sr ^ sc
    strict, incl = sr % c > sc % c, sr % c >= sc % c

    def lower_rows(t, blk):
        return jnp.concatenate([t[:, r0:r0 + blk] for r0 in range(blk, 2 * c, 2 * blk)], axis=1)

    def scatter_lower(t, blk):
        zero = jnp.zeros((t.shape[0], blk, t.shape[2]), t.dtype)
        parts = []
        for i in range(c // blk):
            parts += [zero, t[:, i * blk:(i + 1) * blk]]
        return jnp.concatenate(parts, axis=1)

    def chunk_prep():
        def to_pairs(t):
            return jnp.stack([t[gi * c:(gi + 1) * c, p * PAIR:(p + 1) * PAIR]
                              for gi in range(g) for p in range(n_pairs)])

        cum_p, logw_p, r_p, k_p, v_p, a_p, b_p = (to_pairs(t) for t in (cum, logw, r, k, v, aa, bb))
        e_in = jnp.exp(cum_p)
        e_out = jnp.exp(-cum_p)
        e_prev = jnp.exp(cum_p - logw_p)
        cum_end = cum_p[:, c - 1:c, :]
        e_rest = jnp.exp(cum_end - cum_p)
        lhs = jnp.concatenate([stack_heads(a_p * e_prev), stack_heads(r_p * e_in)], axis=1).astype(BF)
        rhs = jnp.concatenate([stack_heads(b_p * e_out), stack_heads(k_p * e_out)], axis=1).astype(BF)
        bk_rest = jnp.concatenate([stack_heads(b_p * e_rest), stack_heads(k_p * e_rest)], axis=1).astype(BF)
        vs = stack_heads(v_p)
        scores = _bdot_nt(lhs, rhs)
        ab = jnp.where(strict, scores[:, :2 * c, :2 * c], 0.0)
        ak = jnp.where(strict, scores[:, :2 * c, 2 * c:], 0.0).astype(BF)
        rbk = jnp.concatenate([jnp.where(incl, scores[:, 2 * c:, :2 * c], 0.0),
                               jnp.where(incl, scores[:, 2 * c:, 2 * c:], 0.0)], axis=2).astype(BF)
        akv = _bdot(ak, vs.astype(BF))
        n = jnp.where(blk_xor == 1, ab, 0.0)
        blk = 2
        while blk < c:
            in_corner = (blk_xor & -blk) == blk
            n_b = n.astype(BF)
            if blk < F32_SUBLANES:
                off = jnp.where(in_corner, ab, 0.0)
                x = off + _bdot(off.astype(BF), n_b)
                n = n + x + _bdot(n_b, x.astype(BF))
            else:
                off = jnp.where(lower_rows(in_corner[None], blk), lower_rows(ab, blk), 0.0)
                x = off + _bdot(off.astype(BF), n_b)
                corr = x + _bdot(lower_rows(n, blk).astype(BF), scatter_lower(x, blk).astype(BF))
                n = n + scatter_lower(corr, blk)
            blk *= 2
        return lhs, akv, n.astype(BF), rbk, vs, bk_rest, jnp.exp(cum_end)

    def chunk_step(lhs, akv, n_b, rbk, vs, bk_rest, decay_end):
        s0 = s_scr[...]
        ls = _bdot_nt(lhs, s0.astype(BF))
        zz = ls[:, :2 * c] + akv
        u = zz + _bdot(n_b, zz.astype(BF))
        uv = jnp.concatenate([u, vs], axis=1).astype(BF)
        y = ls[:, 2 * c:] + _bdot(rbk, uv)
        s_scr[...] = s0 * decay_end + _bdot_tn(uv, bk_rest)
        y = y[:, :c] + y[:, c:]
        return jnp.concatenate([y[p] for p in range(n_pairs)], axis=1)

    prep = chunk_prep()
    ys = [chunk_step(*(t[gi * n_pairs:(gi + 1) * n_pairs] for t in prep)) for gi in range(g)]
    y = jnp.concatenate(ys, axis=0) if g > 1 else ys[0]

    mean = head_sum(y) * (1.0 / HEAD_DIM)
    dlt = y - mean
    var = head_sum(dlt * dlt) * (1.0 / HEAD_DIM)
    yn = dlt * lax.rsqrt(var + A_GN_EPS) * ln_w + ln_b
    bonus = head_sum(r * k * r_k)
    o_ref[0] = ((yn + bonus * v) * gate).astype(o_ref.dtype)


def rwkv7_mix(pa, mu, wwa, g2, vec):
    b, s, cols = pa.shape
    c = WKV_CHUNK
    rows = min(WKV_GROUP * c, s)
    width = 4 * PAIR
    head = jnp.arange(PAIR, dtype=jnp.int32) // HEAD_DIM
    ones_blk = (head[:, None] == head[None, :]).astype(BF)
    return pl.pallas_call(
        functools.partial(_rwkv_kernel, c=c),
        grid=(b, s // rows),
        in_specs=[pl.BlockSpec((1, rows, cols), lambda i, j: (i, j, 0)),
                  pl.BlockSpec((1, 8, cols), lambda i, j: (i, jnp.maximum(j * (rows // 8) - 1, 0), 0)),
                  _resident((1, cols)), _resident(wwa.shape), _resident(g2.shape), _resident(vec.shape),
                  _resident(ones_blk.shape)],
        out_specs=pl.BlockSpec((1, rows, width), lambda i, j: (i, j, 0)),
        out_shape=jax.ShapeDtypeStruct((b, s, width), BF),
        scratch_shapes=[pltpu.VMEM((4, PAIR, PAIR), F32)],
        compiler_params=_params("arbitrary", "arbitrary"),
        name="rwkv7_mix",
    )(pa, pa, mu.reshape(1, cols), wwa, g2, vec, ones_blk)


def _retention_kernel(pb_ref, rope_ref, tab_ref, dm_ref, lnw_ref, o_ref, s_scr, *, c):
    ci, bi = pl.program_id(0), pl.program_id(1)
    rows = pb_ref.shape[1]
    g = rows // c
    n_pairs = 4
    w = n_pairs * PAIR

    @pl.when(ci == 0)
    def _():
        s_scr[bi] = jnp.zeros(s_scr.shape[1:], F32)

    lane = lax.broadcasted_iota(jnp.int32, (rows, w), 1)
    first_half = (lane % HEAD_DIM) < (HEAD_DIM // 2)
    cos, sin = rope_ref[:, :w], rope_ref[:, w:]

    def rotary(t):
        swapped = jnp.where(first_half, pltpu.roll(t, w - HEAD_DIM // 2, axis=1), pltpu.roll(t, HEAD_DIM // 2, axis=1))
        return t * cos + swapped * sin

    def to_pairs(t):
        return jnp.stack([t[gi * c:(gi + 1) * c, p * PAIR:(p + 1) * PAIR] for gi in range(g) for p in range(n_pairs)])

    def stack_heads(t):
        lo = lax.broadcasted_iota(jnp.int32, t.shape, 2) < HEAD_DIM
        return jnp.concatenate([jnp.where(lo, t, 0.0), jnp.where(lo, 0.0, t)], axis=1)

    q = rotary(pb_ref[0, :, 0:w].astype(F32))
    k = rotary(pb_ref[0, :, w:2 * w].astype(F32)) * (HEAD_DIM ** -0.5)
    gate = pb_ref[0, :, 3 * w:4 * w].astype(F32)
    xi, zeta, cdecay = tab_ref[0], tab_ref[1], tab_ref[2][0:1, :]

    q_p, k_p, qx_p, kz_p = (to_pairs(t) for t in (q, k, q * xi, k * zeta))
    v_p = to_pairs(pb_ref[0, :, 2 * w:3 * w].astype(F32))
    lo = lax.broadcasted_iota(jnp.int32, q_p.shape, 2) < HEAD_DIM
    k_b = k_p.astype(BF)
    dm_lo = jnp.stack([dm_ref[2 * p] for _ in range(g) for p in range(n_pairs)])
    dm_hi = jnp.stack([dm_ref[2 * p + 1] for _ in range(g) for p in range(n_pairs)])
    s_lo = _bdot_nt(jnp.where(lo, q_p, 0.0).astype(BF), k_b) * dm_lo
    s_hi = _bdot_nt(jnp.where(lo, 0.0, q_p).astype(BF), k_b) * dm_hi
    vs = stack_heads(v_p).astype(BF)
    inner = _bdot(jnp.concatenate([s_lo, s_hi], axis=2).astype(BF), vs)
    upd = _bdot_tn(stack_heads(kz_p).astype(BF), vs)

    cd = jnp.stack([cdecay[:, p * PAIR:(p + 1) * PAIR] for p in range(n_pairs)])
    st = s_scr[bi]
    states = []
    for gi in range(g):
        states.append(st)
        st = st * cd + upd[gi * n_pairs:(gi + 1) * n_pairs]
    s_scr[bi] = st
    states = jnp.concatenate(states, axis=0) if g > 1 else states[0]
    o = inner + _bdot(qx_p.astype(BF), states.astype(BF))

    o = jnp.concatenate([jnp.concatenate([o[gi * n_pairs + p] for p in range(n_pairs)], axis=1)
                         for gi in range(g)], axis=0)
    ones_blk = _pair_consts(rows)[1]
    ms = jnp.concatenate([_head_sum((o * o)[:, p * PAIR:(p + 1) * PAIR], ones_blk) for p in range(n_pairs)], axis=1)
    o = o * lax.rsqrt(ms * (1.0 / HEAD_DIM) + NORM_EPS) * lnw_ref[...]
    o_ref[0] = (o * (gate * jax.nn.sigmoid(gate))).astype(o_ref.dtype)


def _retention_tables(s, c, n_heads):
    d = HEAD_DIM
    inv = ROPE_BASE ** (-jnp.arange(0, d, 2, dtype=F32) / d)
    ang = jnp.arange(s, dtype=F32)[:, None] * inv[None, :]
    cos, sin = jnp.cos(ang), jnp.sin(ang)
    cos_full = jnp.tile(jnp.concatenate([cos, cos], axis=1), (1, n_heads))
    sin_signed = jnp.tile(jnp.concatenate([-sin, sin], axis=1), (1, n_heads))
    rope = jnp.concatenate([cos_full, sin_signed], axis=1)
    gamma = 1.0 - 2.0 ** (-5.0 - jnp.arange(n_heads, dtype=F32))
    lg = jnp.log(gamma)[:, None]
    idx = jnp.arange(c, dtype=F32)
    rel = idx[:, None] - idx[None, :]
    dmat = jnp.where(rel >= 0, jnp.exp(jnp.maximum(rel, 0.0)[None] * lg[..., None]), 0.0)
    expand = lambda t: jnp.repeat(t.T, d, axis=1)
    zeta = expand(jnp.exp((c - 1 - idx)[None, :] * lg))
    xi = expand(jnp.exp((idx + 1)[None, :] * lg))
    cdecay = expand(jnp.broadcast_to(jnp.exp(c * lg), (n_heads, c)))
    return rope, jnp.stack([xi, zeta, cdecay]), dmat


def retention_mix(pb, ln_w):
    b, s, cols = pb.shape
    c = RET_CHUNK
    rows = min(RET_GROUP * c, s)
    width = cols // 4
    rope, tab, dmat = _retention_tables(s, c, width // HEAD_DIM)
    tab = jnp.tile(tab, (1, rows // c, 1))
    return pl.pallas_call(
        functools.partial(_retention_kernel, c=c),
        grid=(s // rows, b),
        in_specs=[pl.BlockSpec((1, rows, cols), lambda j, i: (i, j, 0)),
                  pl.BlockSpec((rows, 2 * width), lambda j, i: (j, 0)),
                  _resident(tab.shape), _resident(dmat.shape), _resident((1, width))],
        out_specs=pl.BlockSpec((1, rows, width), lambda j, i: (i, j, 0)),
        out_shape=jax.ShapeDtypeStruct((b, s, width), BF),
        scratch_shapes=[pltpu.VMEM((b, 4, PAIR, PAIR), F32)],
        compiler_params=_params("arbitrary", "arbitrary"),
        name="retention_mix",
    )(pb, rope, tab, dmat, ln_w.reshape(1, width))


def _t5_bucket(rel):
    n = jnp.maximum(rel, 0)
    max_exact = REL_BUCKETS // 2
    large = max_exact + (jnp.log(jnp.maximum(n, max_exact).astype(F32) / max_exact)
                         / math.log(REL_MAX_DIST / max_exact) * (REL_BUCKETS - max_exact)).astype(jnp.int32)
    large = jnp.minimum(large, REL_BUCKETS - 1)
    return jnp.where(n < max_exact, n, large)


def _qkv_proj_kernel(x_ref, g_ref, wqk_ref, wvt_ref, qk_ref, vt_ref, *, n_chunk):
    xn = _rms(x_ref[...], g_ref[...]).astype(BF)
    half = qk_ref.shape[-1] // 2
    for c in range(0, qk_ref.shape[-1], n_chunk):
        scale = ATTN_Q_SCALE if c < half else 1.0
        qk_ref[:, c:c + n_chunk] = (_dot(xn, wqk_ref[:, c:c + n_chunk]) * scale).astype(BF)
    n_heads, n_sub, rows, t = vt_ref.shape[1:]
    vt = _dot_nt(wvt_ref[...], xn)
    for h in range(n_heads):
        for j in range(n_sub):
            vt_ref[0, h, j, :PAIR, :] = vt[h * PAIR:(h + 1) * PAIR, j * t:(j + 1) * t].astype(BF)
            vt_ref[0, h, j, PAIR:, :] = jnp.ones((rows - PAIR, t), BF)


def qkv_proj(x, g, w_qk, w_vt, b, tm=512, n_chunk=512):
    m, k = x.shape
    s = m // b
    tm = min(tm, s)
    t = min(ATTN_TILE, s)
    n_heads = w_vt.shape[0] // PAIR
    per_b = s // tm
    return pl.pallas_call(
        functools.partial(_qkv_proj_kernel, n_chunk=n_chunk),
        grid=(m // tm,),
        in_specs=[pl.BlockSpec((tm, k), lambda i: (i, 0)), _resident((1, k)), _resident(w_qk.shape),
                  _resident(w_vt.shape)],
        out_specs=[pl.BlockSpec((tm, w_qk.shape[1]), lambda i: (i, 0)),
                   pl.BlockSpec((1, n_heads, tm // t, VT_ROWS, t), lambda i: (i // per_b, 0, i % per_b, 0, 0))],
        out_shape=[jax.ShapeDtypeStruct((m, w_qk.shape[1]), BF),
                   jax.ShapeDtypeStruct((b, n_heads, s // t, VT_ROWS, t), BF)],
        compiler_params=_params("parallel"),
        name="qkv_proj",
    )(x, g.reshape(1, k), w_qk, w_vt)


def _attn_schedule(n_q):
    near = [(qi, qi, 0) for qi in range(n_q)] + [(qi - 1, qi, 1) for qi in range(1, n_q)]
    far = [(kj, qi, 2) for qi in range(n_q) for kj in range(qi - 1)]
    entries = [(0, 0, 2)] + near + far
    n_biased = len(near) - 1
    return entries, len(near) + len(far), n_biased + n_biased % 2


def _diff_attn_kernel(sched_ref, relb_ref, lqk_ref, q_ref, k_ref, vt_ref, bkt_ref, lnw_ref, o_ref,
                      bias_scr, m_scr, acc_scr, s_scr, *, lam_init, n_entries, n_bias_steps):
    hi, bi = pl.program_id(0), pl.program_id(1)
    t = bias_scr.shape[1]
    n_q = q_ref.shape[1] // t
    far_bias = relb_ref[REL_BUCKETS - 1, hi]

    @pl.when(bi == 0)
    def _():
        for tile in range(2):
            bkt = bkt_ref[tile]
            bias = jnp.full((t, t), MASK_VALUE, F32)
            for bucket in range(REL_BUCKETS):
                bias = jnp.where(bkt == bucket, (relb_ref[bucket, hi] - far_bias) * LOG2_E, bias)
            bias_scr[tile] = bias
        bias_scr[2] = jnp.zeros((t, t), F32)

    m_scr[...] = jnp.full(m_scr.shape, MASK_VALUE, F32)
    acc_scr[...] = jnp.zeros(acc_scr.shape, F32)
    lane = lax.broadcasted_iota(jnp.int32, (t, PAIR), 1)

    def step(i, slot, with_bias, score=True, absorb=True):
        if score:
            kj, qt, bidx = sched_ref[0, i + 1], sched_ref[1, i + 1], sched_ref[2, i + 1]
            k_t = k_ref[0, pl.ds(pl.multiple_of(kj * t, t), t), :]
            q = q_ref[0, pl.ds(pl.multiple_of(qt * t, t), t), :]
            for c in range(2):
                q_c = jnp.where(lane < HEAD_DIM, q, 0.0) if c == 0 else jnp.where(lane < HEAD_DIM, 0.0, q)
                s = _dot_nt(k_t, q_c.astype(BF))
                s_scr[slot, c] = s + bias_scr[bidx] if with_bias else s
        if absorb:
            st, v_t = sched_ref[1, i], vt_ref[0, 0, sched_ref[0, i]]
            for c in range(2):
                s = s_scr[1 - slot, c]
                m_old = m_scr[st, c]
                m_new = jnp.maximum(m_old, jnp.max(s, axis=0, keepdims=True))
                alpha = jnp.exp2(m_old - m_new)
                p = jnp.exp2(s - m_new).astype(BF)
                acc_scr[st, c] = alpha * acc_scr[st, c] + _dot(v_t, p)
                m_scr[st, c] = m_new

    def two_steps(with_bias):
        def body(j, carry):
            step(2 * j + 1, 1, with_bias)
            step(2 * j + 2, 0, with_bias)
            return carry
        return body

    n_pairs_biased = n_bias_steps // 2
    n_pairs = (n_entries - 1) // 2
    step(0, 0, True, absorb=False)
    lax.fori_loop(0, n_pairs_biased, two_steps(True), 0)
    lax.fori_loop(n_pairs_biased, n_pairs, two_steps(False), 0)
    if (n_entries - 1) % 2:
        step(n_entries - 1, (n_entries - 1) % 2, False)
    step(n_entries, n_entries % 2, False, score=False)

    lam = (jnp.exp(jnp.sum(lqk_ref[0:1, :] * lqk_ref[1:2, :], axis=-1, keepdims=True))
           - jnp.exp(jnp.sum(lqk_ref[2:3, :] * lqk_ref[3:4, :], axis=-1, keepdims=True)) + lam_init)
    for qi in range(n_q):
        acc1, acc2 = acc_scr[qi, 0], acc_scr[qi, 1]
        o_t = (acc1[:PAIR] * (1.0 / acc1[PAIR:PAIR + 1])
               - lam * (acc2[:PAIR] * (1.0 / acc2[PAIR:PAIR + 1])))
        o_t = o_t * lax.rsqrt(jnp.mean(o_t * o_t, axis=0, keepdims=True) + NORM_EPS)
        o_ref[0, qi * t:(qi + 1) * t, :] = (o_t.T * lnw_ref[...] * (1.0 - lam_init)).astype(o_ref.dtype)


def diff_attention(qk, vt, rel_bias, lqk, ln_w, lam_init):
    b, s, cols = qk.shape
    d = cols // 2
    n_heads = d // PAIR
    t = vt.shape[-1]
    n_q = s // t
    pos = jnp.arange(t, dtype=jnp.int32)
    rel = pos[None, :] - pos[:, None]
    buckets = jnp.stack([jnp.where(rel >= 0, _t5_bucket(rel), -1), _t5_bucket(rel + t)])
    entries, n_entries, n_bias_steps = _attn_schedule(n_q)
    sched = jnp.asarray(entries, dtype=jnp.int32).T
    return pl.pallas_call(
        functools.partial(_diff_attn_kernel, lam_init=lam_init, n_entries=n_entries, n_bias_steps=n_bias_steps),
        grid=(n_heads, b),
        in_specs=[pl.BlockSpec(memory_space=pltpu.SMEM),
                  pl.BlockSpec(memory_space=pltpu.SMEM),
                  _resident(lqk.shape),
                  pl.BlockSpec((1, s, PAIR), lambda h, i: (i, 0, h)),
                  pl.BlockSpec((1, s, PAIR), lambda h, i: (i, 0, n_heads + h)),
                  pl.BlockSpec((1, 1, n_q, VT_ROWS, t), lambda h, i: (i, h, 0, 0, 0)),
                  _resident(buckets.shape), _resident((1, PAIR))],
        out_specs=pl.BlockSpec((1, s, PAIR), lambda h, i: (i, 0, h)),
        out_shape=jax.ShapeDtypeStruct((b, s, d), BF),
        scratch_shapes=[pltpu.VMEM((3, t, t), F32),
                        pltpu.VMEM((n_q, 2, 1, t), F32),
                        pltpu.VMEM((n_q, 2, VT_ROWS, t), F32),
                        pltpu.VMEM((2, 2, t, t), F32)],
        compiler_params=_params("arbitrary", "arbitrary"),
        name="diff_attention",
    )(sched, rel_bias, lqk, qk, qk, vt, buckets, ln_w.reshape(1, PAIR))


def kernel(x, mem, rel_bias, mem_norm_w, final_norm_w, norm_mix_w, norm_cross_w, norm_mlp_w, xattn_w_q, xattn_w_kv, xattn_w_o, mlp_w1, mlp_w2, hyb_w_in, rwkv_mu, rwkv_w0, rwkv_w2, rwkv_a0, rwkv_a2, rwkv_g2, rwkv_k_k, rwkv_k_a, rwkv_r_k, rwkv_ln_w, rwkv_ln_b, ret_ln_w, hyb_w_out, diff_w_in, diff_lq1, diff_lk1, diff_lq2, diff_lk2, diff_ln_w, diff_w_out):
    b, s, d = x.shape
    n_mem = mem.shape[1]
    depth = norm_mix_w.shape[0]
    a_width = rwkv_w0.shape[1]
    a_cols = rwkv_mu.shape[1]
    m = b * s
    bf = lambda t: t.astype(BF)

    h = x.reshape(m, d)
    mem2 = mem.reshape(b * n_mem, d)
    for layer in range(depth):
        i = layer // 2
        if layer % 2 == 0:
            pa, pb = norm_matmul(h, norm_mix_w[layer], bf(hyb_w_in[i]),
                                 (a_cols, hyb_w_in.shape[2] - a_cols), (F32, BF))
            lora = rwkv_w2.shape[1]
            zeros = jnp.zeros((lora, a_width), F32)
            wwa = bf(jnp.concatenate([jnp.concatenate([rwkv_w2[i], zeros], axis=1),
                                      jnp.concatenate([zeros, rwkv_a2[i]], axis=1)], axis=0))
            vec = jnp.stack([rwkv_w0[i], rwkv_a0[i], rwkv_k_k[i], rwkv_k_a[i], rwkv_r_k[i].reshape(-1),
                             rwkv_ln_w[i], rwkv_ln_b[i], jnp.zeros((a_width,), F32)])
            y_a = rwkv7_mix(pa.reshape(b, s, -1), rwkv_mu[i], wwa, bf(rwkv_g2[i]), vec)
            y_b = retention_mix(pb.reshape(b, s, -1), ret_ln_w[i])
            w_out = bf(hyb_w_out[i])
            ys, ws = [y_a, y_b], [w_out[:a_width], w_out[a_width:]]
        else:
            lam_init = 0.8 - 0.6 * math.exp(-0.3 * layer)
            qk, vt = qkv_proj(h, norm_mix_w[layer], bf(diff_w_in[i][:, :2 * d]), bf(diff_w_in[i][:, 2 * d:].T), b)
            lqk = jnp.stack([diff_lq1[i], diff_lk1[i], diff_lq2[i], diff_lk2[i]])
            o = diff_attention(qk.reshape(b, s, -1), vt, rel_bias, lqk, diff_ln_w[i], lam_init)
            ys, ws = [o], [bf(diff_w_out[i])]
        (kv,) = norm_matmul(mem2, mem_norm_w, bf(xattn_w_kv[layer]), (2 * d,), (BF,))
        h = xattn_block(ys, ws, h.reshape(b, s, d), norm_cross_w[layer], bf(xattn_w_q[layer]),
                        kv.reshape(b, n_mem, 2 * d), bf(xattn_w_o[layer])).reshape(m, d)
        h = mlp_block(h, norm_mlp_w[layer], bf(mlp_w1[layer]), bf(mlp_w2[layer]), final_norm_w,
                      final_norm=(layer == depth - 1))
    return h.reshape(b, s, d)
```

```python
import functools
import math

import jax
import jax.numpy as jnp
from jax import lax
from jax.experimental import pallas as pl
from jax.experimental.pallas import tpu as pltpu

BF = jnp.bfloat16
F32 = jnp.float32

NORM_EPS = 1e-6
HEAD_DIM = 64
PAIR = 2 * HEAD_DIM
F32_SUBLANES = 8
A_GN_EPS = 64e-5
ROPE_BASE = 10000.0
REL_BUCKETS = 32
REL_MAX_DIST = 128
X_HEADS = 4
WKV_CHUNK = 64
WKV_GROUP = 4
RET_CHUNK = 128
RET_GROUP = 4
ATTN_TILE = 512
ATTN_UNROLL = 10
ATTN_UNROLL_BIASED = 14
VT_ROWS = PAIR + 16
MASK_VALUE = -1e30
LOG2_E = math.log2(math.e)
ATTN_Q_SCALE = HEAD_DIM ** -0.5 * LOG2_E
VMEM_LIMIT_BYTES = 56 * 1024 * 1024


def _dot(a, b):
    return jnp.dot(a, b, preferred_element_type=F32)


def _dot_nt(a, b):
    return lax.dot_general(a, b, (((1,), (1,)), ((), ())), preferred_element_type=F32)


def _dot_tn(a, b):
    return lax.dot_general(a, b, (((0,), (0,)), ((), ())), preferred_element_type=F32)


def _rms(x, g):
    ms = jnp.mean(x * x, axis=-1, keepdims=True)
    return x * lax.rsqrt(ms + NORM_EPS) * g


def _split_bf16(x, parts):
    out = []
    for _ in range(parts):
        hi = x.astype(BF)
        out.append(hi)
        x = x - hi.astype(F32)
    return out


def _head_sum(x, ones_blk):
    return sum(_dot(part, ones_blk) for part in _split_bf16(x, 2))


def _pair_consts(rows):
    lane = lax.broadcasted_iota(jnp.int32, (rows, PAIR), 1)
    lo_mask = lane < HEAD_DIM
    r = lax.broadcasted_iota(jnp.int32, (PAIR, PAIR), 0)
    c = lax.broadcasted_iota(jnp.int32, (PAIR, PAIR), 1)
    ones_blk = jnp.where((r < HEAD_DIM) == (c < HEAD_DIM), 1.0, 0.0).astype(BF)
    return lo_mask, ones_blk


def _stack_heads(x, lo_mask):
    return jnp.concatenate([jnp.where(lo_mask, x, 0.0), jnp.where(lo_mask, 0.0, x)], axis=0)


def _params(*sem):
    return pltpu.CompilerParams(dimension_semantics=sem, vmem_limit_bytes=VMEM_LIMIT_BYTES)


def _resident(shape):
    nd = len(shape)
    return pl.BlockSpec(shape, lambda *_: (0,) * nd, pipeline_mode=pl.Buffered(1))


def _norm_matmul_kernel(x_ref, g_ref, w_ref, *o_refs, n_chunk):
    xn = _rms(x_ref[...], g_ref[...]).astype(BF)
    col = 0
    for o_ref in o_refs:
        n = o_ref.shape[-1]
        for c in range(0, n, n_chunk):
            cc = min(n_chunk, n - c)
            o_ref[:, c:c + cc] = _dot(xn, w_ref[:, col + c:col + c + cc]).astype(o_ref.dtype)
        col += n


def norm_matmul(x, g, w, splits, dtypes, tm=512, n_chunk=512):
    m, k = x.shape
    tm = min(tm, m)
    assert m % tm == 0 and sum(splits) == w.shape[1]
    return pl.pallas_call(
        functools.partial(_norm_matmul_kernel, n_chunk=n_chunk),
        grid=(m // tm,),
        in_specs=[pl.BlockSpec((tm, k), lambda i: (i, 0)), _resident((1, k)), _resident(w.shape)],
        out_specs=[pl.BlockSpec((tm, n), lambda i: (i, 0)) for n in splits],
        out_shape=[jax.ShapeDtypeStruct((m, n), dt) for n, dt in zip(splits, dtypes)],
        compiler_params=_params("parallel"),
        name="norm_matmul",
    )(x, g.reshape(1, k), w)


def _mlp_kernel(h_ref, g_ref, w1_ref, w2_ref, gf_ref, o_ref, a_scr, *, f_chunk, final_norm):
    h = h_ref[...]
    xn = _rms(h, g_ref[...]).astype(BF)
    for c in range(0, a_scr.shape[1], f_chunk):
        a = jnp.maximum(_dot(xn, w1_ref[:, c:c + f_chunk]), 0.0)
        a_scr[:, c:c + f_chunk] = (a * a).astype(BF)
    out = h + _dot(a_scr[...], w2_ref[...])
    if final_norm:
        out = _rms(out, gf_ref[...])
    o_ref[...] = out


def mlp_block(h, g, w1, w2, g_final, final_norm, tm=512, f_chunk=512):
    m, d = h.shape
    tm = min(tm, m)
    f = w1.shape[1]
    return pl.pallas_call(
        functools.partial(_mlp_kernel, f_chunk=f_chunk, final_norm=final_norm),
        grid=(m // tm,),
        in_specs=[pl.BlockSpec((tm, d), lambda i: (i, 0)), _resident((1, d)), _resident(w1.shape),
                  _resident(w2.shape), _resident((1, d))],
        out_specs=pl.BlockSpec((tm, d), lambda i: (i, 0)),
        out_shape=jax.ShapeDtypeStruct((m, d), F32),
        scratch_shapes=[pltpu.VMEM((tm, f), BF)],
        compiler_params=_params("parallel"),
        name="mlp_block",
    )(h, g.reshape(1, d), w1, w2, g_final.reshape(1, d))


def _xattn_kernel(*refs, n_in):
    y_refs, w_refs = refs[:n_in], refs[n_in:2 * n_in]
    h_ref, g_ref, wq_ref, kv_ref, wo_ref, o_ref, a_scr = refs[2 * n_in:]
    h = h_ref[0]
    for y_ref, w_ref in zip(y_refs, w_refs):
        h = h + _dot(y_ref[0], w_ref[...])
    d = h.shape[-1]
    hd = d // X_HEADS
    xn = _rms(h, g_ref[...]).astype(BF)
    q = (_dot(xn, wq_ref[...]) * (hd ** -0.5)).astype(BF)
    for i in range(X_HEADS):
        k = kv_ref[0, :, i * hd:(i + 1) * hd]
        v = kv_ref[0, :, d + i * hd:d + (i + 1) * hd]
        s = _dot_nt(q[:, i * hd:(i + 1) * hd], k)
        p = jnp.exp(s - jnp.max(s, axis=-1, keepdims=True))
        o = _dot(p.astype(BF), v) / jnp.sum(p, axis=-1, keepdims=True)
        a_scr[:, i * hd:(i + 1) * hd] = o.astype(BF)
    o_ref[0] = h + _dot(a_scr[...], wo_ref[...])


def xattn_block(ys, ws, h, g, wq, kv, wo, tm=512):
    b, s, d = h.shape
    tm = min(tm, s)
    n_in = len(ys)
    return pl.pallas_call(
        functools.partial(_xattn_kernel, n_in=n_in),
        grid=(b, s // tm),
        in_specs=([pl.BlockSpec((1, tm, y.shape[2]), lambda i, j: (i, j, 0)) for y in ys]
                  + [_resident(w.shape) for w in ws]
                  + [pl.BlockSpec((1, tm, d), lambda i, j: (i, j, 0)), _resident((1, d)), _resident(wq.shape),
                     pl.BlockSpec((1,) + kv.shape[1:], lambda i, j: (i, 0, 0)), _resident(wo.shape)]),
        out_specs=pl.BlockSpec((1, tm, d), lambda i, j: (i, j, 0)),
        out_shape=jax.ShapeDtypeStruct((b, s, d), F32),
        scratch_shapes=[pltpu.VMEM((tm, d), BF)],
        compiler_params=_params("parallel", "parallel"),
        name="xattn_block",
    )(*ys, *ws, h, g.reshape(1, d), wq, kv, wo)


def _bdot(a, b):
    return lax.dot_general(a, b, (((2,), (1,)), ((0,), (0,))), preferred_element_type=F32)


def _bdot_nt(a, b):
    return lax.dot_general(a, b, (((2,), (2,)), ((0,), (0,))), preferred_element_type=F32)


def _bdot_tn(a, b):
    return lax.dot_general(a, b, (((1,), (1,)), ((0,), (0,))), preferred_element_type=F32)


def _rwkv_kernel(pa_ref, prev_ref, mu_ref, wwa_ref, g2_ref, vec_ref, ones_ref, o_ref, s_scr, *, c):
    ci = pl.program_id(1)
    rows = pa_ref.shape[1]
    g = rows // c
    n_pairs = 4
    w = n_pairs * PAIR

    @pl.when(ci == 0)
    def _():
        s_scr[...] = jnp.zeros_like(s_scr)

    pa = pa_ref[0]
    prev_last = jnp.where(ci == 0, 0.0, prev_ref[0][7:8, :])
    row = lax.broadcasted_iota(jnp.int32, pa.shape, 0)
    shifted = jnp.where(row == 0, prev_last, pltpu.roll(pa, 1, axis=0))
    x = pa + (shifted - pa) * mu_ref[...]

    r, k, v = x[:, 0:w], x[:, w:2 * w], x[:, 2 * w:3 * w]
    wa, gd = x[:, 3 * w:3 * w + PAIR], x[:, 3 * w + PAIR:3 * w + 2 * PAIR]
    w0, a0, k_k, k_a, r_k, ln_w, ln_b = (vec_ref[i:i + 1, :] for i in range(7))

    lane = lax.broadcasted_iota(jnp.int32, (rows, PAIR), 1)
    ones_blk = ones_ref[...]

    def head_sum(t):
        t_b = t.astype(BF)
        return jnp.concatenate([_dot(t_b[:, p * PAIR:(p + 1) * PAIR], ones_blk) for p in range(n_pairs)], axis=1)

    z = jnp.where(lane < HEAD_DIM, jnp.tanh(wa), wa).astype(BF)
    twa = _dot(z, wwa_ref[...])
    neg = -(w0 + twa[:, :w])
    softplus = jnp.maximum(neg, 0.0) + jnp.log(1.0 + jnp.exp(-jnp.abs(neg)))
    logw = -jnp.exp(-softplus - 0.5)
    a = jax.nn.sigmoid(a0 + twa[:, w:])
    gate = _dot(jax.nn.sigmoid(gd).astype(BF), g2_ref[...])

    kk = k * k_k
    kk = kk * lax.rsqrt(jnp.maximum(head_sum(kk * kk), 1e-24))
    k = k * (1.0 + (a - 1.0) * k_a)
    aa = -kk
    bb = kk * a

    tr = lax.broadcasted_iota(jnp.int32, (rows, rows), 0)
    tc = lax.broadcasted_iota(jnp.int32, (rows, rows), 1)
    tril = jnp.where((tr >= tc) & (tr // c == tc // c), 1.0, 0.0).astype(BF)
    cum = sum(_dot(tril, part) for part in _split_bf16(logw, 2))

    def stack_heads(t):
        lo = lax.broadcasted_iota(jnp.int32, t.shape, 2) < HEAD_DIM
        return jnp.concatenate([jnp.where(lo, t, 0.0), jnp.where(lo, 0.0, t)], axis=1)

    sr = lax.broadcasted_iota(jnp.int32, (2 * c, 2 * c), 0)
    sc = lax.broadcasted_iota(jnp.int32, (2 * c, 2 * c), 1)
    blk_xor = sr ^ sc
    strict, incl = sr % c > sc % c, sr % c >= sc % c

    def lower_rows(t, blk):
        return jnp.concatenate([t[:, r0:r0 + blk] for r0 in range(blk, 2 * c, 2 * blk)], axis=1)

    def scatter_lower(t, blk):
        zero = jnp.zeros((t.shape[0], blk, t.shape[2]), t.dtype)
        parts = []
        for i in range(c // blk):
            parts += [zero, t[:, i * blk:(i + 1) * blk]]
        return jnp.concatenate(parts, axis=1)

    def chunk_prep():
        def to_pairs(t):
            return jnp.stack([t[gi * c:(gi + 1) * c, p * PAIR:(p + 1) * PAIR]
                              for gi in range(g) for p in range(n_pairs)])

        cum_p, logw_p, r_p, k_p, v_p, a_p, b_p = (to_pairs(t) for t in (cum, logw, r, k, v, aa, bb))
        e_in = jnp.exp(cum_p)
        e_out = jnp.exp(-cum_p)
        e_prev = jnp.exp(cum_p - logw_p)
        cum_end = cum_p[:, c - 1:c, :]
        e_rest = jnp.exp(cum_end - cum_p)
        a_st, r_st = stack_heads(a_p * e_prev), stack_heads(r_p * e_in)
        b_rest = stack_heads(b_p * e_rest).astype(BF)
        lhs = jnp.concatenate([a_st, r_st], axis=1).astype(BF)
        rhs = jnp.concatenate([stack_heads(b_p * e_out), stack_heads(k_p * e_out)], axis=1).astype(BF)
        bk_rest = jnp.concatenate([b_rest, stack_heads(k_p * e_rest).astype(BF)], axis=1)
        vs = stack_heads(v_p)
        scores = _bdot_nt(lhs, rhs)
        ab = jnp.where(strict, scores[:, :2 * c, :2 * c], 0.0)
        ak = jnp.where(strict, scores[:, :2 * c, 2 * c:], 0.0).astype(BF)
        rb = jnp.where(incl, scores[:, 2 * c:, :2 * c], 0.0).astype(BF)
        rk = jnp.where(incl, scores[:, 2 * c:, 2 * c:], 0.0).astype(BF)
        akv = _bdot(ak, vs.astype(BF))
        n = jnp.where(blk_xor == 1, ab, 0.0)
        blk = 2
        while blk < c:
            in_corner = (blk_xor & -blk) == blk
            n_b = n.astype(BF)
            if blk < F32_SUBLANES:
                off = jnp.where(in_corner, ab, 0.0)
                x = off + _bdot(off.astype(BF), n_b)
                n = n + x + _bdot(n_b, x.astype(BF))
            else:
                off = jnp.where(lower_rows(in_corner[None], blk), lower_rows(ab, blk), 0.0)
                x = off + _bdot(off.astype(BF), n_b)
                corr = x + _bdot(lower_rows(n, blk).astype(BF), scatter_lower(x, blk).astype(BF))
                n = n + scatter_lower(corr, blk)
            blk *= 2
        rbk = jnp.concatenate([rb, rk], axis=2)
        return lhs, akv, n.astype(BF), rbk, vs, bk_rest, jnp.exp(cum_end)

    def chunk_step(lhs, akv, n_b, rbk, vs, bk_rest, decay_end):
        s0 = s_scr[...]
        ls = _bdot_nt(lhs, s0.astype(BF))
        zz = ls[:, :2 * c] + akv
        u = zz + _bdot(n_b, zz.astype(BF))
        uv = jnp.concatenate([u, vs], axis=1).astype(BF)
        y = ls[:, 2 * c:] + _bdot(rbk, uv)
        s_scr[...] = s0 * decay_end + _bdot_tn(uv, bk_rest)
        y = y[:, :c] + y[:, c:]
        return jnp.concatenate([y[p] for p in range(n_pairs)], axis=1)

    prep = chunk_prep()
    ys = [chunk_step(*(t[gi * n_pairs:(gi + 1) * n_pairs] for t in prep)) for gi in range(g)]
    y = jnp.concatenate(ys, axis=0) if g > 1 else ys[0]

    mean = head_sum(y) * (1.0 / HEAD_DIM)
    dlt = y - mean
    var = head_sum(dlt * dlt) * (1.0 / HEAD_DIM)
    yn = dlt * lax.rsqrt(var + A_GN_EPS) * ln_w + ln_b
    bonus = head_sum(r * k * r_k)
    o_ref[0] = ((yn + bonus * v) * gate).astype(o_ref.dtype)


def rwkv7_mix(pa, mu, wwa, g2, vec):
    b, s, cols = pa.shape
    c = WKV_CHUNK
    rows = min(WKV_GROUP * c, s)
    width = 4 * PAIR
    head = jnp.arange(PAIR, dtype=jnp.int32) // HEAD_DIM
    ones_blk = (head[:, None] == head[None, :]).astype(BF)
    return pl.pallas_call(
        functools.partial(_rwkv_kernel, c=c),
        grid=(b, s // rows),
        in_specs=[pl.BlockSpec((1, rows, cols), lambda i, j: (i, j, 0)),
                  pl.BlockSpec((1, 8, cols), lambda i, j: (i, jnp.maximum(j * (rows // 8) - 1, 0), 0)),
                  _resident((1, cols)), _resident(wwa.shape), _resident(g2.shape), _resident(vec.shape),
                  _resident(ones_blk.shape)],
        out_specs=pl.BlockSpec((1, rows, width), lambda i, j: (i, j, 0)),
        out_shape=jax.ShapeDtypeStruct((b, s, width), BF),
        scratch_shapes=[pltpu.VMEM((4, PAIR, PAIR), F32)],
        compiler_params=_params("arbitrary", "arbitrary"),
        name="rwkv7_mix",
    )(pa, pa, mu.reshape(1, cols), wwa, g2, vec, ones_blk)


def _retention_kernel(pb_ref, rope_ref, tab_ref, dm_ref, lnw_ref, o_ref, s_scr, *, c):
    ci, bi = pl.program_id(0), pl.program_id(1)
    rows = pb_ref.shape[1]
    g = rows // c
    n_pairs = 4
    w = n_pairs * PAIR

    @pl.when(ci == 0)
    def _():
        s_scr[bi] = jnp.zeros(s_scr.shape[1:], F32)

    lane = lax.broadcasted_iota(jnp.int32, (rows, w), 1)
    first_half = (lane % HEAD_DIM) < (HEAD_DIM // 2)
    cos, sin = rope_ref[:, :w], rope_ref[:, w:]

    def rotary(t):
        swapped = jnp.where(first_half, pltpu.roll(t, w - HEAD_DIM // 2, axis=1), pltpu.roll(t, HEAD_DIM // 2, axis=1))
        return t * cos + swapped * sin

    def to_pairs(t):
        return jnp.stack([t[gi * c:(gi + 1) * c, p * PAIR:(p + 1) * PAIR] for gi in range(g) for p in range(n_pairs)])

    def stack_heads(t):
        lo = lax.broadcasted_iota(jnp.int32, t.shape, 2) < HEAD_DIM
        return jnp.concatenate([jnp.where(lo, t, 0.0), jnp.where(lo, 0.0, t)], axis=1)

    q = rotary(pb_ref[0, :, 0:w].astype(F32))
    k = rotary(pb_ref[0, :, w:2 * w].astype(F32)) * (HEAD_DIM ** -0.5)
    gate = pb_ref[0, :, 3 * w:4 * w].astype(F32)
    xi, zeta, cdecay = tab_ref[0], tab_ref[1], tab_ref[2][0:1, :]

    q_p, k_p, qx_p, kz_p = (to_pairs(t) for t in (q, k, q * xi, k * zeta))
    v_p = to_pairs(pb_ref[0, :, 2 * w:3 * w].astype(F32))
    lo = lax.broadcasted_iota(jnp.int32, q_p.shape, 2) < HEAD_DIM
    k_b = k_p.astype(BF)
    dm_lo = jnp.stack([dm_ref[2 * p] for _ in range(g) for p in range(n_pairs)])
    dm_hi = jnp.stack([dm_ref[2 * p + 1] for _ in range(g) for p in range(n_pairs)])
    s_lo = _bdot_nt(jnp.where(lo, q_p, 0.0).astype(BF), k_b) * dm_lo
    s_hi = _bdot_nt(jnp.where(lo, 0.0, q_p).astype(BF), k_b) * dm_hi
    vs = stack_heads(v_p).astype(BF)
    inner = _bdot(jnp.concatenate([s_lo, s_hi], axis=2).astype(BF), vs)
    upd = _bdot_tn(stack_heads(kz_p).astype(BF), vs)

    cd = jnp.stack([cdecay[:, p * PAIR:(p + 1) * PAIR] for p in range(n_pairs)])
    st = s_scr[bi]
    states = []
    for gi in range(g):
        states.append(st)
        st = st * cd + upd[gi * n_pairs:(gi + 1) * n_pairs]
    s_scr[bi] = st
    states = jnp.concatenate(states, axis=0) if g > 1 else states[0]
    o = inner + _bdot(qx_p.astype(BF), states.astype(BF))

    o = jnp.concatenate([jnp.concatenate([o[gi * n_pairs + p] for p in range(n_pairs)], axis=1)
                         for gi in range(g)], axis=0)
    ones_blk = _pair_consts(rows)[1]
    ms = jnp.concatenate([_head_sum((o * o)[:, p * PAIR:(p + 1) * PAIR], ones_blk) for p in range(n_pairs)], axis=1)
    o = o * lax.rsqrt(ms * (1.0 / HEAD_DIM) + NORM_EPS) * lnw_ref[...]
    o_ref[0] = (o * (gate * jax.nn.sigmoid(gate))).astype(o_ref.dtype)


def _retention_tables(s, c, n_heads):
    d = HEAD_DIM
    inv = ROPE_BASE ** (-jnp.arange(0, d, 2, dtype=F32) / d)
    ang = jnp.arange(s, dtype=F32)[:, None] * inv[None, :]
    cos, sin = jnp.cos(ang), jnp.sin(ang)
    cos_full = jnp.tile(jnp.concatenate([cos, cos], axis=1), (1, n_heads))
    sin_signed = jnp.tile(jnp.concatenate([-sin, sin], axis=1), (1, n_heads))
    rope = jnp.concatenate([cos_full, sin_signed], axis=1)
    gamma = 1.0 - 2.0 ** (-5.0 - jnp.arange(n_heads, dtype=F32))
    lg = jnp.log(gamma)[:, None]
    idx = jnp.arange(c, dtype=F32)
    rel = idx[:, None] - idx[None, :]
    dmat = jnp.where(rel >= 0, jnp.exp(jnp.maximum(rel, 0.0)[None] * lg[..., None]), 0.0)
    expand = lambda t: jnp.repeat(t.T, d, axis=1)
    zeta = expand(jnp.exp((c - 1 - idx)[None, :] * lg))
    xi = expand(jnp.exp((idx + 1)[None, :] * lg))
    cdecay = expand(jnp.broadcast_to(jnp.exp(c * lg), (n_heads, c)))
    return rope, jnp.stack([xi, zeta, cdecay]), dmat


def retention_mix(pb, ln_w):
    b, s, cols = pb.shape
    c = RET_CHUNK
    rows = min(RET_GROUP * c, s)
    width = cols // 4
    rope, tab, dmat = _retention_tables(s, c, width // HEAD_DIM)
    tab = jnp.tile(tab, (1, rows // c, 1))
    return pl.pallas_call(
        functools.partial(_retention_kernel, c=c),
        grid=(s // rows, b),
        in_specs=[pl.BlockSpec((1, rows, cols), lambda j, i: (i, j, 0)),
                  pl.BlockSpec((rows, 2 * width), lambda j, i: (j, 0)),
                  _resident(tab.shape), _resident(dmat.shape), _resident((1, width))],
        out_specs=pl.BlockSpec((1, rows, width), lambda j, i: (i, j, 0)),
        out_shape=jax.ShapeDtypeStruct((b, s, width), BF),
        scratch_shapes=[pltpu.VMEM((b, 4, PAIR, PAIR), F32)],
        compiler_params=_params("arbitrary", "arbitrary"),
        name="retention_mix",
    )(pb, rope, tab, dmat, ln_w.reshape(1, width))


def _t5_bucket(rel):
    n = jnp.maximum(rel, 0)
    max_exact = REL_BUCKETS // 2
    large = max_exact + (jnp.log(jnp.maximum(n, max_exact).astype(F32) / max_exact)
                         / math.log(REL_MAX_DIST / max_exact) * (REL_BUCKETS - max_exact)).astype(jnp.int32)
    large = jnp.minimum(large, REL_BUCKETS - 1)
    return jnp.where(n < max_exact, n, large)


def _qkv_proj_kernel(x_ref, g_ref, wqk_ref, wvt_ref, qk_ref, vt_ref, *, n_chunk):
    xn = _rms(x_ref[...], g_ref[...]).astype(BF)
    half = qk_ref.shape[-1] // 2
    for c in range(0, qk_ref.shape[-1], n_chunk):
        scale = ATTN_Q_SCALE if c < half else 1.0
        qk_ref[:, c:c + n_chunk] = (_dot(xn, wqk_ref[:, c:c + n_chunk]) * scale).astype(BF)
    n_heads, n_sub, rows, t = vt_ref.shape[1:]
    vt = _dot_nt(wvt_ref[...], xn)
    for h in range(n_heads):
        for j in range(n_sub):
            vt_ref[0, h, j, :PAIR, :] = vt[h * PAIR:(h + 1) * PAIR, j * t:(j + 1) * t].astype(BF)
            vt_ref[0, h, j, PAIR:, :] = jnp.ones((rows - PAIR, t), BF)


def qkv_proj(x, g, w_qk, w_vt, b, tm=512, n_chunk=512):
    m, k = x.shape
    s = m // b
    tm = min(tm, s)
    t = min(ATTN_TILE, s)
    n_heads = w_vt.shape[0] // PAIR
    per_b = s // tm
    return pl.pallas_call(
        functools.partial(_qkv_proj_kernel, n_chunk=n_chunk),
        grid=(m // tm,),
        in_specs=[pl.BlockSpec((tm, k), lambda i: (i, 0)), _resident((1, k)), _resident(w_qk.shape),
                  _resident(w_vt.shape)],
        out_specs=[pl.BlockSpec((tm, w_qk.shape[1]), lambda i: (i, 0)),
                   pl.BlockSpec((1, n_heads, tm // t, VT_ROWS, t), lambda i: (i // per_b, 0, i % per_b, 0, 0))],
        out_shape=[jax.ShapeDtypeStruct((m, w_qk.shape[1]), BF),
                   jax.ShapeDtypeStruct((b, n_heads, s // t, VT_ROWS, t), BF)],
        compiler_params=_params("parallel"),
        name="qkv_proj",
    )(x, g.reshape(1, k), w_qk, w_vt)


def _attn_schedule(n_q):
    near = [(qi, qi, 0) for qi in range(n_q)] + [(qi - 1, qi, 1) for qi in range(1, n_q)]
    far = [(kj, qi, 2) for qi in range(n_q) for kj in range(qi - 1)]
    entries = [(0, 0, 2)] + near + far
    n_biased = len(near) - 1
    return entries, len(near) + len(far), n_biased + n_biased % 2


def _diff_attn_kernel(sched_ref, relb_ref, lqk_ref, q_ref, k_ref, vt_ref, bkt_ref, lnw_ref, o_ref,
                      bias_scr, m_scr, acc_scr, s_scr, *, lam_init, n_entries, n_bias_steps):
    hi, bi = pl.program_id(0), pl.program_id(1)
    t = bias_scr.shape[1]
    n_q = q_ref.shape[1] // t
    far_bias = relb_ref[REL_BUCKETS - 1, hi]

    @pl.when(bi == 0)
    def _():
        for tile in range(2):
            bkt = bkt_ref[tile]
            bias = jnp.full((t, t), MASK_VALUE, F32)
            for bucket in range(REL_BUCKETS):
                bias = jnp.where(bkt == bucket, (relb_ref[bucket, hi] - far_bias) * LOG2_E, bias)
            bias_scr[tile] = bias
        bias_scr[2] = jnp.zeros((t, t), F32)

    m_scr[...] = jnp.full(m_scr.shape, MASK_VALUE, F32)
    acc_scr[...] = jnp.zeros(acc_scr.shape, F32)
    lane = lax.broadcasted_iota(jnp.int32, (t, PAIR), 1)

    def step(i, slot, with_bias, score=True, absorb=True):
        if score:
            kj, qt, bidx = sched_ref[0, i + 1], sched_ref[1, i + 1], sched_ref[2, i + 1]
            k_t = k_ref[0, pl.ds(pl.multiple_of(kj * t, t), t), :]
            q = q_ref[0, pl.ds(pl.multiple_of(qt * t, t), t), :]
        if absorb:
            st, v_t = sched_ref[1, i], vt_ref[0, 0, sched_ref[0, i]]
        for c in range(2):
            if score:
                q_c = jnp.where(lane < HEAD_DIM, q, 0.0) if c == 0 else jnp.where(lane < HEAD_DIM, 0.0, q)
                s = _dot_nt(k_t, q_c.astype(BF))
                s_scr[slot, c] = s + bias_scr[bidx] if with_bias else s
            if absorb:
                s = s_scr[1 - slot, c]
                m_old = m_scr[st, c]
                m_new = jnp.maximum(m_old, jnp.max(s, axis=0, keepdims=True))
                alpha = jnp.exp2(m_old - m_new)
                p = jnp.exp2(s - m_new).astype(BF)
                acc_scr[st, c] = alpha * acc_scr[st, c] + _dot(v_t, p)
                m_scr[st, c] = m_new

    def unrolled(with_bias, first, k):
        def body(j, carry):
            for r in range(k):
                step(first + k * j + r, (first + r) % 2, with_bias)
            return carry
        return body

    step(0, 0, True, absorb=False)
    k_bias = max(d for d in range(2, ATTN_UNROLL_BIASED + 1, 2) if n_bias_steps % d == 0) if n_bias_steps else 2
    lax.fori_loop(0, n_bias_steps // k_bias, unrolled(True, 1, k_bias), 0)
    first_far, n_far_steps = n_bias_steps + 1, n_entries - 1 - n_bias_steps
    lax.fori_loop(0, n_far_steps // ATTN_UNROLL, unrolled(False, first_far, ATTN_UNROLL), 0)
    for i in range(n_entries - n_far_steps % ATTN_UNROLL, n_entries):
        step(i, i % 2, False)
    step(n_entries, n_entries % 2, False, score=False)

    lam = (jnp.exp(jnp.sum(lqk_ref[0:1, :] * lqk_ref[1:2, :], axis=-1, keepdims=True))
           - jnp.exp(jnp.sum(lqk_ref[2:3, :] * lqk_ref[3:4, :], axis=-1, keepdims=True)) + lam_init)
    for qi in range(n_q):
        acc1, acc2 = acc_scr[qi, 0], acc_scr[qi, 1]
        o_t = (acc1[:PAIR] * (1.0 / acc1[PAIR:PAIR + 1])
               - lam * (acc2[:PAIR] * (1.0 / acc2[PAIR:PAIR + 1])))
        o_t = o_t * lax.rsqrt(jnp.mean(o_t * o_t, axis=0, keepdims=True) + NORM_EPS)
        o_ref[0, qi * t:(qi + 1) * t, :] = (o_t.T * lnw_ref[...] * (1.0 - lam_init)).astype(o_ref.dtype)


def diff_attention(qk, vt, rel_bias, lqk, ln_w, lam_init):
    b, s, cols = qk.shape
    d = cols // 2
    n_heads = d // PAIR
    t = vt.shape[-1]
    n_q = s // t
    pos = jnp.arange(t, dtype=jnp.int32)
    rel = pos[None, :] - pos[:, None]
    buckets = jnp.stack([jnp.where(rel >= 0, _t5_bucket(rel), -1), _t5_bucket(rel + t)])
    entries, n_entries, n_bias_steps = _attn_schedule(n_q)
    sched = jnp.asarray(entries, dtype=jnp.int32).T
    return pl.pallas_call(
        functools.partial(_diff_attn_kernel, lam_init=lam_init, n_entries=n_entries, n_bias_steps=n_bias_steps),
        grid=(n_heads, b),
        in_specs=[pl.BlockSpec(memory_space=pltpu.SMEM),
                  pl.BlockSpec(memory_space=pltpu.SMEM),
                  _resident(lqk.shape),
                  pl.BlockSpec((1, s, PAIR), lambda h, i: (i, 0, h)),
                  pl.BlockSpec((1, s, PAIR), lambda h, i: (i, 0, n_heads + h)),
                  pl.BlockSpec((1, 1, n_q, VT_ROWS, t), lambda h, i: (i, h, 0, 0, 0)),
                  _resident(buckets.shape), _resident((1, PAIR))],
        out_specs=pl.BlockSpec((1, s, PAIR), lambda h, i: (i, 0, h)),
        out_shape=jax.ShapeDtypeStruct((b, s, d), BF),
        scratch_shapes=[pltpu.VMEM((3, t, t), F32),
                        pltpu.VMEM((n_q, 2, 1, t), F32),
                        pltpu.VMEM((n_q, 2, VT_ROWS, t), F32),
                        pltpu.VMEM((2, 2, t, t), F32)],
        compiler_params=_params("arbitrary", "arbitrary"),
        name="diff_attention",
    )(sched, rel_bias, lqk, qk, qk, vt, buckets, ln_w.reshape(1, PAIR))


def kernel(x, mem, rel_bias, mem_norm_w, final_norm_w, norm_mix_w, norm_cross_w, norm_mlp_w, xattn_w_q, xattn_w_kv, xattn_w_o, mlp_w1, mlp_w2, hyb_w_in, rwkv_mu, rwkv_w0, rwkv_w2, rwkv_a0, rwkv_a2, rwkv_g2, rwkv_k_k, rwkv_k_a, rwkv_r_k, rwkv_ln_w, rwkv_ln_b, ret_ln_w, hyb_w_out, diff_w_in, diff_lq1, diff_lk1, diff_lq2, diff_lk2, diff_ln_w, diff_w_out):
    b, s, d = x.shape
    n_mem = mem.shape[1]
    depth = norm_mix_w.shape[0]
    a_width = rwkv_w0.shape[1]
    a_cols = rwkv_mu.shape[1]
    m = b * s
    bf = lambda t: t.astype(BF)

    h = x.reshape(m, d)
    mem2 = mem.reshape(b * n_mem, d)
    for layer in range(depth):
        i = layer // 2
        if layer % 2 == 0:
            pa, pb = norm_matmul(h, norm_mix_w[layer], bf(hyb_w_in[i]),
                                 (a_cols, hyb_w_in.shape[2] - a_cols), (F32, BF))
            lora = rwkv_w2.shape[1]
            zeros = jnp.zeros((lora, a_width), F32)
            wwa = bf(jnp.concatenate([jnp.concatenate([rwkv_w2[i], zeros], axis=1),
                                      jnp.concatenate([zeros, rwkv_a2[i]], axis=1)], axis=0))
            vec = jnp.stack([rwkv_w0[i], rwkv_a0[i], rwkv_k_k[i], rwkv_k_a[i], rwkv_r_k[i].reshape(-1),
                             rwkv_ln_w[i], rwkv_ln_b[i], jnp.zeros((a_width,), F32)])
            y_a = rwkv7_mix(pa.reshape(b, s, -1), rwkv_mu[i], wwa, bf(rwkv_g2[i]), vec)
            y_b = retention_mix(pb.reshape(b, s, -1), ret_ln_w[i])
            w_out = bf(hyb_w_out[i])
            ys, ws = [y_a, y_b], [w_out[:a_width], w_out[a_width:]]
        else:
            lam_init = 0.8 - 0.6 * math.exp(-0.3 * layer)
            qk, vt = qkv_proj(h, norm_mix_w[layer], bf(diff_w_in[i][:, :2 * d]), bf(diff_w_in[i][:, 2 * d:].T), b)
            lqk = jnp.stack([diff_lq1[i], diff_lk1[i], diff_lq2[i], diff_lk2[i]])
            o = diff_attention(qk.reshape(b, s, -1), vt, rel_bias, lqk, diff_ln_w[i], lam_init)
            ys, ws = [o], [bf(diff_w_out[i])]
        (kv,) = norm_matmul(mem2, mem_norm_w, bf(xattn_w_kv[layer]), (2 * d,), (BF,))
        h = xattn_block(ys, ws, h.reshape(b, s, d), norm_cross_w[layer], bf(xattn_w_q[layer]),
                        kv.reshape(b, n_mem, 2 * d), bf(xattn_w_o[layer])).reshape(m, d)
        h = mlp_block(h, norm_mlp_w[layer], bf(mlp_w1[layer]), bf(mlp_w2[layer]), final_norm_w,
                      final_norm=(layer == depth - 1))
    return h.reshape(b, s, d)
```

```python
import functools
import math
from typing import NamedTuple

import jax
import jax.numpy as jnp
from jax import lax
from jax.experimental import pallas as pl
from jax.experimental.pallas import tpu as pltpu

BF = jnp.bfloat16
F32 = jnp.float32

NORM_EPS = 1e-6
HEAD_DIM = 64
PAIR = 2 * HEAD_DIM
F32_SUBLANES = 8
A_GN_EPS = 64e-5
ROPE_BASE = 10000.0
REL_BUCKETS = 32
REL_MAX_DIST = 128
X_HEADS = 4
WKV_CHUNK = 64
WKV_GROUP = 4
RET_CHUNK = 128
RET_GROUP = 4
ATTN_TILE = 512
ATTN_UNROLL = 10
ATTN_UNROLL_BIASED = 14
VT_ROWS = PAIR + 16
MASK_VALUE = -1e30
LOG2_E = math.log2(math.e)
ATTN_Q_SCALE = HEAD_DIM ** -0.5 * LOG2_E
VMEM_LIMIT_BYTES = 56 * 1024 * 1024


def _dot(a, b):
    return jnp.dot(a, b, preferred_element_type=F32)


def _dot_nt(a, b):
    return lax.dot_general(a, b, (((1,), (1,)), ((), ())), preferred_element_type=F32)


def _dot_tn(a, b):
    return lax.dot_general(a, b, (((0,), (0,)), ((), ())), preferred_element_type=F32)


def _rms(x, g):
    ms = jnp.mean(x * x, axis=-1, keepdims=True)
    return x * lax.rsqrt(ms + NORM_EPS) * g


def _split_bf16(x, parts):
    out = []
    for _ in range(parts):
        hi = x.astype(BF)
        out.append(hi)
        x = x - hi.astype(F32)
    return out


def _head_sum(x, ones_blk):
    return sum(_dot(part, ones_blk) for part in _split_bf16(x, 2))


def _pair_consts(rows):
    lane = lax.broadcasted_iota(jnp.int32, (rows, PAIR), 1)
    lo_mask = lane < HEAD_DIM
    r = lax.broadcasted_iota(jnp.int32, (PAIR, PAIR), 0)
    c = lax.broadcasted_iota(jnp.int32, (PAIR, PAIR), 1)
    ones_blk = jnp.where((r < HEAD_DIM) == (c < HEAD_DIM), 1.0, 0.0).astype(BF)
    return lo_mask, ones_blk


def _stack_heads(x, lo_mask):
    return jnp.concatenate([jnp.where(lo_mask, x, 0.0), jnp.where(lo_mask, 0.0, x)], axis=0)


def _params(*sem):
    return pltpu.CompilerParams(dimension_semantics=sem, vmem_limit_bytes=VMEM_LIMIT_BYTES)


def _resident(shape):
    nd = len(shape)
    return pl.BlockSpec(shape, lambda *_: (0,) * nd, pipeline_mode=pl.Buffered(1))


class _Weight(NamedTuple):
    array: jax.Array
    block: tuple
    index: tuple

    @property
    def shape(self):
        return tuple(d for d in self.block if d is not None)

    def spec(self):
        index = self.index
        return pl.BlockSpec(self.block, lambda *_: index, pipeline_mode=pl.Buffered(1))


def _whole(array):
    return _Weight(array, array.shape, (0,) * array.ndim)


def _of_layer(stacked, layer, rows=None, row_block=0, cols=None):
    return _Weight(stacked, (None, rows or stacked.shape[1], cols or stacked.shape[2]), (layer, row_block, 0))


def _norm_matmul_kernel(x_ref, g_ref, w_ref, *o_refs, n_chunk):
    xn = _rms(x_ref[...], g_ref[...]).astype(BF)
    col = 0
    for o_ref in o_refs:
        n = o_ref.shape[-1]
        for c in range(0, n, n_chunk):
            cc = min(n_chunk, n - c)
            o_ref[:, c:c + cc] = _dot(xn, w_ref[:, col + c:col + c + cc]).astype(o_ref.dtype)
        col += n


def norm_matmul(x, g, w, splits, dtypes, tm=512, n_chunk=512):
    m, k = x.shape
    tm = min(tm, m)
    assert m % tm == 0 and sum(splits) == w.shape[1]
    return pl.pallas_call(
        functools.partial(_norm_matmul_kernel, n_chunk=n_chunk),
        grid=(m // tm,),
        in_specs=[pl.BlockSpec((tm, k), lambda i: (i, 0)), _resident((1, k)), w.spec()],
        out_specs=[pl.BlockSpec((tm, n), lambda i: (i, 0)) for n in splits],
        out_shape=[jax.ShapeDtypeStruct((m, n), dt) for n, dt in zip(splits, dtypes)],
        compiler_params=_params("parallel"),
        name="norm_matmul",
    )(x, g.reshape(1, k), w.array)


def _mlp_kernel(h_ref, g_ref, w1_ref, w2_ref, gf_ref, o_ref, a_scr, *, f_chunk, final_norm):
    h = h_ref[...]
    xn = _rms(h, g_ref[...]).astype(BF)
    for c in range(0, a_scr.shape[1], f_chunk):
        a = jnp.maximum(_dot(xn, w1_ref[:, c:c + f_chunk]), 0.0)
        a_scr[:, c:c + f_chunk] = (a * a).astype(BF)
    out = h + _dot(a_scr[...], w2_ref[...])
    if final_norm:
        out = _rms(out, gf_ref[...])
    o_ref[...] = out


def mlp_block(h, g, w1, w2, g_final, final_norm, tm=512, f_chunk=512):
    m, d = h.shape
    tm = min(tm, m)
    f = w1.shape[1]
    return pl.pallas_call(
        functools.partial(_mlp_kernel, f_chunk=f_chunk, final_norm=final_norm),
        grid=(m // tm,),
        in_specs=[pl.BlockSpec((tm, d), lambda i: (i, 0)), _resident((1, d)), w1.spec(), w2.spec(),
                  _resident((1, d))],
        out_specs=pl.BlockSpec((tm, d), lambda i: (i, 0)),
        out_shape=jax.ShapeDtypeStruct((m, d), F32),
        scratch_shapes=[pltpu.VMEM((tm, f), BF)],
        compiler_params=_params("parallel"),
        name="mlp_block",
    )(h, g.reshape(1, d), w1.array, w2.array, g_final.reshape(1, d))


def _xattn_kernel(*refs, n_in):
    y_refs, w_refs = refs[:n_in], refs[n_in:2 * n_in]
    h_ref, g_ref, wq_ref, kv_ref, wo_ref, o_ref, a_scr = refs[2 * n_in:]
    h = h_ref[0]
    for y_ref, w_ref in zip(y_refs, w_refs):
        h = h + _dot(y_ref[0], w_ref[...])
    d = h.shape[-1]
    hd = d // X_HEADS
    xn = _rms(h, g_ref[...]).astype(BF)
    q = (_dot(xn, wq_ref[...]) * (hd ** -0.5)).astype(BF)
    for i in range(X_HEADS):
        k = kv_ref[0, :, i * hd:(i + 1) * hd]
        v = kv_ref[0, :, d + i * hd:d + (i + 1) * hd]
        s = _dot_nt(q[:, i * hd:(i + 1) * hd], k)
        p = jnp.exp(s - jnp.max(s, axis=-1, keepdims=True))
        o = _dot(p.astype(BF), v) / jnp.sum(p, axis=-1, keepdims=True)
        a_scr[:, i * hd:(i + 1) * hd] = o.astype(BF)
    o_ref[0] = h + _dot(a_scr[...], wo_ref[...])


def xattn_block(ys, ws, h, g, wq, kv, wo, tm=512):
    b, s, d = h.shape
    tm = min(tm, s)
    n_in = len(ys)
    return pl.pallas_call(
        functools.partial(_xattn_kernel, n_in=n_in),
        grid=(b, s // tm),
        in_specs=([pl.BlockSpec((1, tm, y.shape[2]), lambda i, j: (i, j, 0)) for y in ys]
                  + [w.spec() for w in ws]
                  + [pl.BlockSpec((1, tm, d), lambda i, j: (i, j, 0)), _resident((1, d)), wq.spec(),
                     pl.BlockSpec((1,) + kv.shape[1:], lambda i, j: (i, 0, 0)), wo.spec()]),
        out_specs=pl.BlockSpec((1, tm, d), lambda i, j: (i, j, 0)),
        out_shape=jax.ShapeDtypeStruct((b, s, d), F32),
        scratch_shapes=[pltpu.VMEM((tm, d), BF)],
        compiler_params=_params("parallel", "parallel"),
        name="xattn_block",
    )(*ys, *(w.array for w in ws), h, g.reshape(1, d), wq.array, kv, wo.array)


def _bdot(a, b):
    return lax.dot_general(a, b, (((2,), (1,)), ((0,), (0,))), preferred_element_type=F32)


def _bdot_nt(a, b):
    return lax.dot_general(a, b, (((2,), (2,)), ((0,), (0,))), preferred_element_type=F32)


def _bdot_tn(a, b):
    return lax.dot_general(a, b, (((1,), (1,)), ((0,), (0,))), preferred_element_type=F32)


def _rwkv_kernel(pa_ref, prev_ref, mu_ref, wwa_ref, g2_ref, vec_ref, ones_ref, o_ref, s_scr, *, c):
    ci = pl.program_id(1)
    rows = pa_ref.shape[1]
    g = rows // c
    n_pairs = 4
    w = n_pairs * PAIR

    @pl.when(ci == 0)
    def _():
        s_scr[...] = jnp.zeros_like(s_scr)

    pa = pa_ref[0]
    prev_last = jnp.where(ci == 0, 0.0, prev_ref[0][7:8, :])
    row = lax.broadcasted_iota(jnp.int32, pa.shape, 0)
    shifted = jnp.where(row == 0, prev_last, pltpu.roll(pa, 1, axis=0))
    x = pa + (shifted - pa) * mu_ref[...]

    r, k, v = x[:, 0:w], x[:, w:2 * w], x[:, 2 * w:3 * w]
    wa, gd = x[:, 3 * w:3 * w + PAIR], x[:, 3 * w + PAIR:3 * w + 2 * PAIR]
    w0, a0, k_k, k_a, r_k, ln_w, ln_b = (vec_ref[i:i + 1, :] for i in range(7))

    lane = lax.broadcasted_iota(jnp.int32, (rows, PAIR), 1)
    ones_blk = ones_ref[...]

    def head_sum(t):
        t_b = t.astype(BF)
        return jnp.concatenate([_dot(t_b[:, p * PAIR:(p + 1) * PAIR], ones_blk) for p in range(n_pairs)], axis=1)

    lora_in = jnp.where(lane < HEAD_DIM, jnp.tanh(wa), wa).astype(BF)
    twa = _dot(lora_in, wwa_ref[...])
    neg = -(w0 + twa[:, :w])
    softplus = jnp.maximum(neg, 0.0) + jnp.log(1.0 + jnp.exp(-jnp.abs(neg)))
    logw = -jnp.exp(-softplus - 0.5)
    a = jax.nn.sigmoid(a0 + twa[:, w:])
    gate = _dot(jax.nn.sigmoid(gd).astype(BF), g2_ref[...])

    kk = k * k_k
    kk = kk * lax.rsqrt(jnp.maximum(head_sum(kk * kk), 1e-24))
    k = k * (1.0 + (a - 1.0) * k_a)
    aa = -kk
    bb = kk * a

    tr = lax.broadcasted_iota(jnp.int32, (rows, rows), 0)
    tc = lax.broadcasted_iota(jnp.int32, (rows, rows), 1)
    tril = jnp.where((tr >= tc) & (tr // c == tc // c), 1.0, 0.0).astype(BF)
    cum = sum(_dot(tril, part) for part in _split_bf16(logw, 2))

    def stack_heads(t):
        lo = lax.broadcasted_iota(jnp.int32, t.shape, 2) < HEAD_DIM
        return jnp.concatenate([jnp.where(lo, t, 0.0), jnp.where(lo, 0.0, t)], axis=1)

    sr = lax.broadcasted_iota(jnp.int32, (2 * c, 2 * c), 0)
    sc = lax.broadcasted_iota(jnp.int32, (2 * c, 2 * c), 1)
    blk_xor = sr ^ sc
    strict, incl = sr % c > sc % c, sr % c >= sc % c

    def lower_rows(t, blk):
        return jnp.concatenate([t[:, r0:r0 + blk] for r0 in range(blk, 2 * c, 2 * blk)], axis=1)

    def scatter_lower(t, blk):
        zero = jnp.zeros((t.shape[0], blk, t.shape[2]), t.dtype)
        parts = []
        for i in range(c // blk):
            parts += [zero, t[:, i * blk:(i + 1) * blk]]
        return jnp.concatenate(parts, axis=1)

    def chunk_prep():
        def to_pairs(t):
            return jnp.stack([t[gi * c:(gi + 1) * c, p * PAIR:(p + 1) * PAIR]
                              for gi in range(g) for p in range(n_pairs)])

        cum_p, logw_p, r_p, k_p, v_p, a_p, b_p = (to_pairs(t) for t in (cum, logw, r, k, v, aa, bb))
        e_in = jnp.exp(cum_p)
        e_out = jnp.exp(-cum_p)
        e_prev = jnp.exp(cum_p - logw_p)
        cum_end = cum_p[:, c - 1:c, :]
        e_rest = jnp.exp(cum_end - cum_p)
        a_st, r_st = stack_heads(a_p * e_prev), stack_heads(r_p * e_in)
        b_rest = stack_heads(b_p * e_rest).astype(BF)
        lhs = jnp.concatenate([a_st, r_st], axis=1).astype(BF)
        rhs = jnp.concatenate([stack_heads(b_p * e_out), stack_heads(k_p * e_out)], axis=1).astype(BF)
        bk_rest = jnp.concatenate([b_rest, stack_heads(k_p * e_rest).astype(BF)], axis=1)
        vs = stack_heads(v_p)
        scores = _bdot_nt(lhs, rhs)
        ab = jnp.where(strict, scores[:, :2 * c, :2 * c], 0.0)
        ak = jnp.where(strict, scores[:, :2 * c, 2 * c:], 0.0).astype(BF)
        rb = jnp.where(incl, scores[:, 2 * c:, :2 * c], 0.0).astype(BF)
        rk = jnp.where(incl, scores[:, 2 * c:, 2 * c:], 0.0).astype(BF)
        akv = _bdot(ak, vs.astype(BF))
        n = jnp.where(blk_xor == 1, ab, 0.0)
        blk = 2
        while blk < c:
            in_corner = (blk_xor & -blk) == blk
            n_b = n.astype(BF)
            if blk < F32_SUBLANES:
                off = jnp.where(in_corner, ab, 0.0)
                x = off + _bdot(off.astype(BF), n_b)
                n = n + x + _bdot(n_b, x.astype(BF))
            else:
                off = jnp.where(lower_rows(in_corner[None], blk), lower_rows(ab, blk), 0.0)
                x = off + _bdot(off.astype(BF), n_b)
                corr = x + _bdot(lower_rows(n, blk).astype(BF), scatter_lower(x, blk).astype(BF))
                n = n + scatter_lower(corr, blk)
            blk *= 2
        rbk = jnp.concatenate([rb, rk], axis=2)
        return lhs, akv, n.astype(BF), rbk, vs, bk_rest, jnp.exp(cum_end)

    def chunk_step(lhs, akv, n_b, rbk, vs, bk_rest, decay_end):
        s0 = s_scr[...]
        ls = _bdot_nt(lhs, s0.astype(BF))
        zz = ls[:, :2 * c] + akv
        u = zz + _bdot(n_b, zz.astype(BF))
        uv = jnp.concatenate([u, vs], axis=1).astype(BF)
        y = ls[:, 2 * c:] + _bdot(rbk, uv)
        s_scr[...] = s0 * decay_end + _bdot_tn(uv, bk_rest)
        y = y[:, :c] + y[:, c:]
        return jnp.concatenate([y[p] for p in range(n_pairs)], axis=1)

    prep = chunk_prep()
    ys = [chunk_step(*(t[gi * n_pairs:(gi + 1) * n_pairs] for t in prep)) for gi in range(g)]
    y = jnp.concatenate(ys, axis=0) if g > 1 else ys[0]

    mean = head_sum(y) * (1.0 / HEAD_DIM)
    dlt = y - mean
    var = head_sum(dlt * dlt) * (1.0 / HEAD_DIM)
    yn = dlt * lax.rsqrt(var + A_GN_EPS) * ln_w + ln_b
    bonus = head_sum(r * k * r_k)
    o_ref[0] = ((yn + bonus * v) * gate).astype(o_ref.dtype)


def rwkv7_mix(pa, mu, wwa, g2, vec):
    b, s, cols = pa.shape
    c = WKV_CHUNK
    rows = min(WKV_GROUP * c, s)
    width = 4 * PAIR
    head = jnp.arange(PAIR, dtype=jnp.int32) // HEAD_DIM
    ones_blk = (head[:, None] == head[None, :]).astype(BF)
    return pl.pallas_call(
        functools.partial(_rwkv_kernel, c=c),
        grid=(b, s // rows),
        in_specs=[pl.BlockSpec((1, rows, cols), lambda i, j: (i, j, 0)),
                  pl.BlockSpec((1, 8, cols), lambda i, j: (i, jnp.maximum(j * (rows // 8) - 1, 0), 0)),
                  _resident((1, cols)), _resident(wwa.shape), _resident(g2.shape), _resident(vec.shape),
                  _resident(ones_blk.shape)],
        out_specs=pl.BlockSpec((1, rows, width), lambda i, j: (i, j, 0)),
        out_shape=jax.ShapeDtypeStruct((b, s, width), BF),
        scratch_shapes=[pltpu.VMEM((4, PAIR, PAIR), F32)],
        compiler_params=_params("arbitrary", "arbitrary"),
        name="rwkv7_mix",
    )(pa, pa, mu.reshape(1, cols), wwa, g2, vec, ones_blk)


def _retention_kernel(pb_ref, rope_ref, tab_ref, dm_ref, lnw_ref, o_ref, s_scr, *, c):
    ci, bi = pl.program_id(0), pl.program_id(1)
    rows = pb_ref.shape[1]
    g = rows // c
    n_pairs = 4
    w = n_pairs * PAIR

    @pl.when(ci == 0)
    def _():
        s_scr[bi] = jnp.zeros(s_scr.shape[1:], F32)

    lane = lax.broadcasted_iota(jnp.int32, (rows, w), 1)
    first_half = (lane % HEAD_DIM) < (HEAD_DIM // 2)
    cos, sin = rope_ref[:, :w], rope_ref[:, w:]

    def rotary(t):
        swapped = jnp.where(first_half, pltpu.roll(t, w - HEAD_DIM // 2, axis=1), pltpu.roll(t, HEAD_DIM // 2, axis=1))
        return t * cos + swapped * sin

    def to_pairs(t):
        return jnp.stack([t[gi * c:(gi + 1) * c, p * PAIR:(p + 1) * PAIR] for gi in range(g) for p in range(n_pairs)])

    def stack_heads(t):
        lo = lax.broadcasted_iota(jnp.int32, t.shape, 2) < HEAD_DIM
        return jnp.concatenate([jnp.where(lo, t, 0.0), jnp.where(lo, 0.0, t)], axis=1)

    q = rotary(pb_ref[0, :, 0:w].astype(F32))
    k = rotary(pb_ref[0, :, w:2 * w].astype(F32)) * (HEAD_DIM ** -0.5)
    gate = pb_ref[0, :, 3 * w:4 * w].astype(F32)
    xi, zeta, cdecay = tab_ref[0], tab_ref[1], tab_ref[2][0:1, :]

    q_p, k_p, qx_p, kz_p = (to_pairs(t) for t in (q, k, q * xi, k * zeta))
    v_p = to_pairs(pb_ref[0, :, 2 * w:3 * w].astype(F32))
    lo = lax.broadcasted_iota(jnp.int32, q_p.shape, 2) < HEAD_DIM
    k_b = k_p.astype(BF)
    dm_lo = jnp.stack([dm_ref[2 * p] for _ in range(g) for p in range(n_pairs)])
    dm_hi = jnp.stack([dm_ref[2 * p + 1] for _ in range(g) for p in range(n_pairs)])
    s_lo = _bdot_nt(jnp.where(lo, q_p, 0.0).astype(BF), k_b) * dm_lo
    s_hi = _bdot_nt(jnp.where(lo, 0.0, q_p).astype(BF), k_b) * dm_hi
    vs = stack_heads(v_p).astype(BF)
    inner = _bdot(jnp.concatenate([s_lo, s_hi], axis=2).astype(BF), vs)
    upd = _bdot_tn(stack_heads(kz_p).astype(BF), vs)

    cd = jnp.stack([cdecay[:, p * PAIR:(p + 1) * PAIR] for p in range(n_pairs)])
    st = s_scr[bi]
    states = []
    for gi in range(g):
        states.append(st)
        st = st * cd + upd[gi * n_pairs:(gi + 1) * n_pairs]
    s_scr[bi] = st
    states = jnp.concatenate(states, axis=0) if g > 1 else states[0]
    o = inner + _bdot(qx_p.astype(BF), states.astype(BF))

    o = jnp.concatenate([jnp.concatenate([o[gi * n_pairs + p] for p in range(n_pairs)], axis=1)
                         for gi in range(g)], axis=0)
    ones_blk = _pair_consts(rows)[1]
    ms = jnp.concatenate([_head_sum((o * o)[:, p * PAIR:(p + 1) * PAIR], ones_blk) for p in range(n_pairs)], axis=1)
    o = o * lax.rsqrt(ms * (1.0 / HEAD_DIM) + NORM_EPS) * lnw_ref[...]
    o_ref[0] = (o * (gate * jax.nn.sigmoid(gate))).astype(o_ref.dtype)


def _retention_tables(s, c, n_heads):
    d = HEAD_DIM
    inv = ROPE_BASE ** (-jnp.arange(0, d, 2, dtype=F32) / d)
    ang = jnp.arange(s, dtype=F32)[:, None] * inv[None, :]
    cos, sin = jnp.cos(ang), jnp.sin(ang)
    cos_full = jnp.tile(jnp.concatenate([cos, cos], axis=1), (1, n_heads))
    sin_signed = jnp.tile(jnp.concatenate([-sin, sin], axis=1), (1, n_heads))
    rope = jnp.concatenate([cos_full, sin_signed], axis=1)
    gamma = 1.0 - 2.0 ** (-5.0 - jnp.arange(n_heads, dtype=F32))
    lg = jnp.log(gamma)[:, None]
    idx = jnp.arange(c, dtype=F32)
    rel = idx[:, None] - idx[None, :]
    dmat = jnp.where(rel >= 0, jnp.exp(jnp.maximum(rel, 0.0)[None] * lg[..., None]), 0.0)
    expand = lambda t: jnp.repeat(t.T, d, axis=1)
    zeta = expand(jnp.exp((c - 1 - idx)[None, :] * lg))
    xi = expand(jnp.exp((idx + 1)[None, :] * lg))
    cdecay = expand(jnp.broadcast_to(jnp.exp(c * lg), (n_heads, c)))
    return rope, jnp.stack([xi, zeta, cdecay]), dmat


def retention_mix(pb, ln_w):
    b, s, cols = pb.shape
    c = RET_CHUNK
    rows = min(RET_GROUP * c, s)
    width = cols // 4
    rope, tab, dmat = _retention_tables(s, c, width // HEAD_DIM)
    tab = jnp.tile(tab, (1, rows // c, 1))
    return pl.pallas_call(
        functools.partial(_retention_kernel, c=c),
        grid=(s // rows, b),
        in_specs=[pl.BlockSpec((1, rows, cols), lambda j, i: (i, j, 0)),
                  pl.BlockSpec((rows, 2 * width), lambda j, i: (j, 0)),
                  _resident(tab.shape), _resident(dmat.shape), _resident((1, width))],
        out_specs=pl.BlockSpec((1, rows, width), lambda j, i: (i, j, 0)),
        out_shape=jax.ShapeDtypeStruct((b, s, width), BF),
        scratch_shapes=[pltpu.VMEM((b, 4, PAIR, PAIR), F32)],
        compiler_params=_params("arbitrary", "arbitrary"),
        name="retention_mix",
    )(pb, rope, tab, dmat, ln_w.reshape(1, width))


def _t5_bucket(rel):
    n = jnp.maximum(rel, 0)
    max_exact = REL_BUCKETS // 2
    large = max_exact + (jnp.log(jnp.maximum(n, max_exact).astype(F32) / max_exact)
                         / math.log(REL_MAX_DIST / max_exact) * (REL_BUCKETS - max_exact)).astype(jnp.int32)
    large = jnp.minimum(large, REL_BUCKETS - 1)
    return jnp.where(n < max_exact, n, large)


def _qkv_proj_kernel(x_ref, g_ref, wqk_ref, wvt_ref, qk_ref, vt_ref, *, n_chunk):
    xn = _rms(x_ref[...], g_ref[...]).astype(BF)
    half = qk_ref.shape[-1] // 2
    for c in range(0, qk_ref.shape[-1], n_chunk):
        scale = ATTN_Q_SCALE if c < half else 1.0
        qk_ref[:, c:c + n_chunk] = (_dot(xn, wqk_ref[:, c:c + n_chunk]) * scale).astype(BF)
    n_heads, n_sub, rows, t = vt_ref.shape[1:]
    vt = _dot_nt(wvt_ref[...], xn)
    for h in range(n_heads):
        for j in range(n_sub):
            vt_ref[0, h, j, :PAIR, :] = vt[h * PAIR:(h + 1) * PAIR, j * t:(j + 1) * t].astype(BF)
            vt_ref[0, h, j, PAIR:, :] = jnp.ones((rows - PAIR, t), BF)


def qkv_proj(x, g, w_qk, w_vt, b, tm=512, n_chunk=512):
    m, k = x.shape
    s = m // b
    tm = min(tm, s)
    t = min(ATTN_TILE, s)
    n_heads = w_vt.shape[0] // PAIR
    per_b = s // tm
    return pl.pallas_call(
        functools.partial(_qkv_proj_kernel, n_chunk=n_chunk),
        grid=(m // tm,),
        in_specs=[pl.BlockSpec((tm, k), lambda i: (i, 0)), _resident((1, k)), w_qk.spec(), w_vt.spec()],
        out_specs=[pl.BlockSpec((tm, w_qk.shape[1]), lambda i: (i, 0)),
                   pl.BlockSpec((1, n_heads, tm // t, VT_ROWS, t), lambda i: (i // per_b, 0, i % per_b, 0, 0))],
        out_shape=[jax.ShapeDtypeStruct((m, w_qk.shape[1]), BF),
                   jax.ShapeDtypeStruct((b, n_heads, s // t, VT_ROWS, t), BF)],
        compiler_params=_params("parallel"),
        name="qkv_proj",
    )(x, g.reshape(1, k), w_qk.array, w_vt.array)


def _attn_schedule(n_q):
    near = [(qi, qi, 0) for qi in range(n_q)] + [(qi - 1, qi, 1) for qi in range(1, n_q)]
    far = [(kj, qi, 2) for qi in range(n_q) for kj in range(qi - 1)]
    entries = [(0, 0, 2)] + near + far
    n_biased = len(near) - 1
    return entries, len(near) + len(far), n_biased + n_biased % 2


def _diff_attn_kernel(sched_ref, relb_ref, lqk_ref, q_ref, k_ref, vt_ref, bkt_ref, lnw_ref, o_ref,
                      bias_scr, m_scr, acc_scr, s_scr, *, lam_init, entries, n_entries, n_bias_steps):
    hi, bi = pl.program_id(0), pl.program_id(1)
    t = bias_scr.shape[1]
    n_q = q_ref.shape[1] // t
    far_bias = relb_ref[REL_BUCKETS - 1, hi]

    @pl.when(bi == 0)
    def _():
        for tile in range(2):
            bkt = bkt_ref[tile]
            bias = jnp.full((t, t), MASK_VALUE, F32)
            for bucket in range(REL_BUCKETS):
                bias = jnp.where(bkt == bucket, (relb_ref[bucket, hi] - far_bias) * LOG2_E, bias)
            bias_scr[tile] = bias
        bias_scr[2] = jnp.zeros((t, t), F32)

    k_bias = max(d for d in range(2, ATTN_UNROLL_BIASED + 1, 2) if n_bias_steps % d == 0) if n_bias_steps else 2
    static_first = n_bias_steps // k_bias <= 1
    if not static_first:
        m_scr[...] = jnp.full(m_scr.shape, MASK_VALUE, F32)
        acc_scr[...] = jnp.zeros(acc_scr.shape, F32)
    lane = lax.broadcasted_iota(jnp.int32, (t, PAIR), 1)

    def entry(i):
        if isinstance(i, int):
            return entries[i]
        return sched_ref[0, i], sched_ref[1, i], sched_ref[2, i]

    def tile(ref, j):
        return ref[0, j * t:(j + 1) * t, :] if isinstance(j, int) else ref[0, pl.ds(pl.multiple_of(j * t, t), t), :]

    def step(i, slot, with_bias, score=True, absorb=True):
        if score:
            kj, qt, bidx = entry(i + 1)
            k_t, q = tile(k_ref, kj), tile(q_ref, qt)
        if absorb:
            kj_i, st, bidx_i = entry(i)
            v_t = vt_ref[0, 0, kj_i]
            first = static_first and isinstance(i, int) and bidx_i == 0
        for c in range(2):
            if score:
                q_c = jnp.where(lane < HEAD_DIM, q, 0.0) if c == 0 else jnp.where(lane < HEAD_DIM, 0.0, q)
                s = _dot_nt(k_t, q_c.astype(BF))
                s_scr[slot, c] = s + bias_scr[bidx] if with_bias else s
            if absorb:
                s = s_scr[1 - slot, c]
                m_new = jnp.max(s, axis=0, keepdims=True)
                if not first:
                    m_old = m_scr[st, c]
                    m_new = jnp.maximum(m_old, m_new)
                pv = _dot(v_t, jnp.exp2(s - m_new).astype(BF))
                acc_scr[st, c] = pv if first else jnp.exp2(m_old - m_new) * acc_scr[st, c] + pv
                m_scr[st, c] = m_new

    def run(n_iter, first, k, with_bias):
        def body(j, carry):
            for r in range(k):
                step(first + k * j + r, (first + r) % 2, with_bias)
            return carry
        if n_iter == 1:
            body(0, 0)
        elif n_iter > 1:
            lax.fori_loop(0, n_iter, body, 0)

    step(0, 0, True, absorb=False)
    run(n_bias_steps // k_bias, 1, k_bias, True)
    first_far, n_far_steps = n_bias_steps + 1, n_entries - 1 - n_bias_steps
    run(n_far_steps // ATTN_UNROLL, first_far, ATTN_UNROLL, False)
    for i in range(n_entries - n_far_steps % ATTN_UNROLL, n_entries):
        step(i, i % 2, False)
    step(n_entries, n_entries % 2, False, score=False)

    lam = (jnp.exp(jnp.sum(lqk_ref[0:1, :] * lqk_ref[1:2, :], axis=-1, keepdims=True))
           - jnp.exp(jnp.sum(lqk_ref[2:3, :] * lqk_ref[3:4, :], axis=-1, keepdims=True)) + lam_init)
    for qi in range(n_q):
        acc1, acc2 = acc_scr[qi, 0], acc_scr[qi, 1]
        o_t = (acc1[:PAIR] * (1.0 / acc1[PAIR:PAIR + 1])
               - lam * (acc2[:PAIR] * (1.0 / acc2[PAIR:PAIR + 1])))
        o_t = o_t * lax.rsqrt(jnp.mean(o_t * o_t, axis=0, keepdims=True) + NORM_EPS)
        o_ref[0, qi * t:(qi + 1) * t, :] = (o_t.T * lnw_ref[...] * (1.0 - lam_init)).astype(o_ref.dtype)


def diff_attention(qk, vt, rel_bias, lqk, ln_w, lam_init):
    b, s, cols = qk.shape
    d = cols // 2
    n_heads = d // PAIR
    t = vt.shape[-1]
    n_q = s // t
    pos = jnp.arange(t, dtype=jnp.int32)
    rel = pos[None, :] - pos[:, None]
    buckets = jnp.stack([jnp.where(rel >= 0, _t5_bucket(rel), -1), _t5_bucket(rel + t)])
    entries, n_entries, n_bias_steps = _attn_schedule(n_q)
    sched = jnp.asarray(entries, dtype=jnp.int32).T
    return pl.pallas_call(
        functools.partial(_diff_attn_kernel, lam_init=lam_init, entries=tuple(entries), n_entries=n_entries,
                          n_bias_steps=n_bias_steps),
        grid=(n_heads, b),
        in_specs=[pl.BlockSpec(memory_space=pltpu.SMEM),
                  pl.BlockSpec(memory_space=pltpu.SMEM),
                  _resident(lqk.shape),
                  pl.BlockSpec((1, s, PAIR), lambda h, i: (i, 0, h)),
                  pl.BlockSpec((1, s, PAIR), lambda h, i: (i, 0, n_heads + h)),
                  pl.BlockSpec((1, 1, n_q, VT_ROWS, t), lambda h, i: (i, h, 0, 0, 0)),
                  _resident(buckets.shape), _resident((1, PAIR))],
        out_specs=pl.BlockSpec((1, s, PAIR), lambda h, i: (i, 0, h)),
        out_shape=jax.ShapeDtypeStruct((b, s, d), BF),
        scratch_shapes=[pltpu.VMEM((3, t, t), F32),
                        pltpu.VMEM((n_q, 2, 1, t), F32),
                        pltpu.VMEM((n_q, 2, VT_ROWS, t), F32),
                        pltpu.VMEM((2, 2, t, t), F32)],
        compiler_params=_params("arbitrary", "arbitrary"),
        name="diff_attention",
    )(sched, rel_bias, lqk, qk, qk, vt, buckets, ln_w.reshape(1, PAIR))


def kernel(x, mem, rel_bias, mem_norm_w, final_norm_w, norm_mix_w, norm_cross_w, norm_mlp_w, xattn_w_q, xattn_w_kv, xattn_w_o, mlp_w1, mlp_w2, hyb_w_in, rwkv_mu, rwkv_w0, rwkv_w2, rwkv_a0, rwkv_a2, rwkv_g2, rwkv_k_k, rwkv_k_a, rwkv_r_k, rwkv_ln_w, rwkv_ln_b, ret_ln_w, hyb_w_out, diff_w_in, diff_lq1, diff_lk1, diff_lq2, diff_lk2, diff_ln_w, diff_w_out):
    b, s, d = x.shape
    n_mem = mem.shape[1]
    depth = norm_mix_w.shape[0]
    a_width = rwkv_w0.shape[1]
    a_cols = rwkv_mu.shape[1]
    m = b * s
    bf = lambda t: t.astype(BF)
    hyb_w_in, hyb_w_out, diff_w_out = bf(hyb_w_in), bf(hyb_w_out), bf(diff_w_out)
    diff_w_qk = bf(diff_w_in[:, :, :2 * d])
    xattn_w_q, xattn_w_kv, xattn_w_o = bf(xattn_w_q), bf(xattn_w_kv), bf(xattn_w_o)
    mlp_w1, mlp_w2 = bf(mlp_w1), bf(mlp_w2)

    h = x.reshape(m, d)
    mem2 = mem.reshape(b * n_mem, d)
    for layer in range(depth):
        i = layer // 2
        if layer % 2 == 0:
            pa, pb = norm_matmul(h, norm_mix_w[layer], _of_layer(hyb_w_in, i),
                                 (a_cols, hyb_w_in.shape[2] - a_cols), (F32, BF))
            lora = rwkv_w2.shape[1]
            zeros = jnp.zeros((lora, a_width), F32)
            wwa = bf(jnp.concatenate([jnp.concatenate([rwkv_w2[i], zeros], axis=1),
                                      jnp.concatenate([zeros, rwkv_a2[i]], axis=1)], axis=0))
            vec = jnp.stack([rwkv_w0[i], rwkv_a0[i], rwkv_k_k[i], rwkv_k_a[i], rwkv_r_k[i].reshape(-1),
                             rwkv_ln_w[i], rwkv_ln_b[i], jnp.zeros((a_width,), F32)])
            y_a = rwkv7_mix(pa.reshape(b, s, -1), rwkv_mu[i], wwa, bf(rwkv_g2[i]), vec)
            y_b = retention_mix(pb.reshape(b, s, -1), ret_ln_w[i])
            ys = [y_a, y_b]
            ws = [_of_layer(hyb_w_out, i, rows=a_width, row_block=0), _of_layer(hyb_w_out, i, rows=a_width, row_block=1)]
        else:
            lam_init = 0.8 - 0.6 * math.exp(-0.3 * layer)
            qk, vt = qkv_proj(h, norm_mix_w[layer], _of_layer(diff_w_qk, i),
                              _whole(bf(diff_w_in[i][:, 2 * d:].T)), b)
            lqk = jnp.stack([diff_lq1[i], diff_lk1[i], diff_lq2[i], diff_lk2[i]])
            o = diff_attention(qk.reshape(b, s, -1), vt, rel_bias, lqk, diff_ln_w[i], lam_init)
            ys, ws = [o], [_of_layer(diff_w_out, i)]
        (kv,) = norm_matmul(mem2, mem_norm_w, _of_layer(xattn_w_kv, layer), (2 * d,), (BF,))
        h = xattn_block(ys, ws, h.reshape(b, s, d), norm_cross_w[layer], _of_layer(xattn_w_q, layer),
                        kv.reshape(b, n_mem, 2 * d), _of_layer(xattn_w_o, layer)).reshape(m, d)
        h = mlp_block(h, norm_mlp_w[layer], _of_layer(mlp_w1, layer), _of_layer(mlp_w2, layer), final_norm_w,
                      final_norm=(layer == depth - 1))
    return h.reshape(b, s, d)
```

```python
import functools
import math
from typing import NamedTuple

import jax
import jax.numpy as jnp
from jax import lax
from jax.experimental import pallas as pl
from jax.experimental.pallas import tpu as pltpu

BF = jnp.bfloat16
F32 = jnp.float32

NORM_EPS = 1e-6
HEAD_DIM = 64
PAIR = 2 * HEAD_DIM
F32_SUBLANES = 8
A_GN_EPS = 64e-5
ROPE_BASE = 10000.0
REL_BUCKETS = 32
REL_MAX_DIST = 128
X_HEADS = 4
WKV_CHUNK = 64
WKV_GROUP = 4
RET_CHUNK = 128
RET_GROUP = 4
ATTN_TILE = 512
ATTN_UNROLL = 10
ATTN_UNROLL_BIASED = 14
VT_ROWS = PAIR + 16
MASK_VALUE = -1e30
LOG2_E = math.log2(math.e)
ATTN_Q_SCALE = HEAD_DIM ** -0.5 * LOG2_E
VMEM_LIMIT_BYTES = 56 * 1024 * 1024


def _dot(a, b):
    return jnp.dot(a, b, preferred_element_type=F32)


def _dot_nt(a, b):
    return lax.dot_general(a, b, (((1,), (1,)), ((), ())), preferred_element_type=F32)


def _dot_tn(a, b):
    return lax.dot_general(a, b, (((0,), (0,)), ((), ())), preferred_element_type=F32)


def _rms(x, g):
    ms = jnp.mean(x * x, axis=-1, keepdims=True)
    return x * lax.rsqrt(ms + NORM_EPS) * g


def _split_bf16(x, parts):
    out = []
    for _ in range(parts):
        hi = x.astype(BF)
        out.append(hi)
        x = x - hi.astype(F32)
    return out


def _head_sum(x, ones_blk):
    return sum(_dot(part, ones_blk) for part in _split_bf16(x, 2))


def _pair_consts(rows):
    lane = lax.broadcasted_iota(jnp.int32, (rows, PAIR), 1)
    lo_mask = lane < HEAD_DIM
    r = lax.broadcasted_iota(jnp.int32, (PAIR, PAIR), 0)
    c = lax.broadcasted_iota(jnp.int32, (PAIR, PAIR), 1)
    ones_blk = jnp.where((r < HEAD_DIM) == (c < HEAD_DIM), 1.0, 0.0).astype(BF)
    return lo_mask, ones_blk


def _stack_heads(x, lo_mask):
    return jnp.concatenate([jnp.where(lo_mask, x, 0.0), jnp.where(lo_mask, 0.0, x)], axis=0)


def _params(*sem):
    return pltpu.CompilerParams(dimension_semantics=sem, vmem_limit_bytes=VMEM_LIMIT_BYTES)


def _resident(shape):
    nd = len(shape)
    return pl.BlockSpec(shape, lambda *_: (0,) * nd, pipeline_mode=pl.Buffered(1))


class _Weight(NamedTuple):
    array: jax.Array
    block: tuple
    index: tuple

    @property
    def shape(self):
        return tuple(d for d in self.block if d is not None)

    def spec(self):
        index = self.index
        return pl.BlockSpec(self.block, lambda *_: index, pipeline_mode=pl.Buffered(1))


def _whole(array):
    return _Weight(array, array.shape, (0,) * array.ndim)


def _of_layer(stacked, layer, rows=None, row_block=0, cols=None):
    return _Weight(stacked, (None, rows or stacked.shape[1], cols or stacked.shape[2]), (layer, row_block, 0))


def _norm_matmul_kernel(x_ref, g_ref, w_ref, *o_refs, n_chunk):
    xn = _rms(x_ref[...], g_ref[...]).astype(BF)
    col = 0
    for o_ref in o_refs:
        n = o_ref.shape[-1]
        for c in range(0, n, n_chunk):
            cc = min(n_chunk, n - c)
            o_ref[:, c:c + cc] = _dot(xn, w_ref[:, col + c:col + c + cc]).astype(o_ref.dtype)
        col += n


def norm_matmul(x, g, w, splits, dtypes, tm=1024, n_chunk=512):
    m, k = x.shape
    tm = min(tm, m)
    assert m % tm == 0 and sum(splits) == w.shape[1]
    return pl.pallas_call(
        functools.partial(_norm_matmul_kernel, n_chunk=n_chunk),
        grid=(m // tm,),
        in_specs=[pl.BlockSpec((tm, k), lambda i: (i, 0)), _resident((1, k)), w.spec()],
        out_specs=[pl.BlockSpec((tm, n), lambda i: (i, 0)) for n in splits],
        out_shape=[jax.ShapeDtypeStruct((m, n), dt) for n, dt in zip(splits, dtypes)],
        compiler_params=_params("parallel"),
        name="norm_matmul",
    )(x, g.reshape(1, k), w.array)


def _mlp_kernel(h_ref, g_ref, w1_ref, w2_ref, gf_ref, o_ref, a_scr, *, f_chunk, final_norm):
    h = h_ref[...]
    xn = _rms(h, g_ref[...]).astype(BF)
    for c in range(0, a_scr.shape[1], f_chunk):
        a = jnp.maximum(_dot(xn, w1_ref[:, c:c + f_chunk]), 0.0)
        a_scr[:, c:c + f_chunk] = (a * a).astype(BF)
    out = h + _dot(a_scr[...], w2_ref[...])
    if final_norm:
        out = _rms(out, gf_ref[...])
    o_ref[...] = out


def mlp_block(h, g, w1, w2, g_final, final_norm, tm=512, f_chunk=512):
    m, d = h.shape
    tm = min(tm, m)
    f = w1.shape[1]
    return pl.pallas_call(
        functools.partial(_mlp_kernel, f_chunk=f_chunk, final_norm=final_norm),
        grid=(m // tm,),
        in_specs=[pl.BlockSpec((tm, d), lambda i: (i, 0)), _resident((1, d)), w1.spec(), w2.spec(),
                  _resident((1, d))],
        out_specs=pl.BlockSpec((tm, d), lambda i: (i, 0)),
        out_shape=jax.ShapeDtypeStruct((m, d), F32),
        scratch_shapes=[pltpu.VMEM((tm, f), BF)],
        compiler_params=_params("parallel"),
        name="mlp_block",
    )(h, g.reshape(1, d), w1.array, w2.array, g_final.reshape(1, d))


def _xattn_kernel(*refs, n_in):
    y_refs, w_refs = refs[:n_in], refs[n_in:2 * n_in]
    h_ref, g_ref, wq_ref, kv_ref, wo_ref, o_ref, a_scr = refs[2 * n_in:]
    h = h_ref[0]
    for y_ref, w_ref in zip(y_refs, w_refs):
        h = h + _dot(y_ref[0], w_ref[...])
    d = h.shape[-1]
    hd = d // X_HEADS
    xn = _rms(h, g_ref[...]).astype(BF)
    q = (_dot(xn, wq_ref[...]) * (hd ** -0.5)).astype(BF)
    for i in range(X_HEADS):
        k = kv_ref[0, :, i * hd:(i + 1) * hd]
        v = kv_ref[0, :, d + i * hd:d + (i + 1) * hd]
        s = _dot_nt(q[:, i * hd:(i + 1) * hd], k)
        p = jnp.exp(s - jnp.max(s, axis=-1, keepdims=True))
        o = _dot(p.astype(BF), v) / jnp.sum(p, axis=-1, keepdims=True)
        a_scr[:, i * hd:(i + 1) * hd] = o.astype(BF)
    o_ref[0] = h + _dot(a_scr[...], wo_ref[...])


def xattn_block(ys, ws, h, g, wq, kv, wo, tm=1024):
    b, s, d = h.shape
    tm = min(tm, s)
    n_in = len(ys)
    return pl.pallas_call(
        functools.partial(_xattn_kernel, n_in=n_in),
        grid=(b, s // tm),
        in_specs=([pl.BlockSpec((1, tm, y.shape[2]), lambda i, j: (i, j, 0)) for y in ys]
                  + [w.spec() for w in ws]
                  + [pl.BlockSpec((1, tm, d), lambda i, j: (i, j, 0)), _resident((1, d)), wq.spec(),
                     pl.BlockSpec((1,) + kv.shape[1:], lambda i, j: (i, 0, 0)), wo.spec()]),
        out_specs=pl.BlockSpec((1, tm, d), lambda i, j: (i, j, 0)),
        out_shape=jax.ShapeDtypeStruct((b, s, d), F32),
        scratch_shapes=[pltpu.VMEM((tm, d), BF)],
        compiler_params=_params("parallel", "parallel"),
        name="xattn_block",
    )(*ys, *(w.array for w in ws), h, g.reshape(1, d), wq.array, kv, wo.array)


def _bdot(a, b):
    return lax.dot_general(a, b, (((2,), (1,)), ((0,), (0,))), preferred_element_type=F32)


def _bdot_nt(a, b):
    return lax.dot_general(a, b, (((2,), (2,)), ((0,), (0,))), preferred_element_type=F32)


def _bdot_tn(a, b):
    return lax.dot_general(a, b, (((1,), (1,)), ((0,), (0,))), preferred_element_type=F32)


def _rwkv_kernel(pa_ref, prev_ref, mu_ref, wwa_ref, g2_ref, vec_ref, ones_ref, o_ref, s_scr, *, c):
    ci = pl.program_id(1)
    rows = pa_ref.shape[1]
    g = rows // c
    n_pairs = 4
    w = n_pairs * PAIR

    @pl.when(ci == 0)
    def _():
        s_scr[...] = jnp.zeros_like(s_scr)

    pa = pa_ref[0]
    prev_last = jnp.where(ci == 0, 0.0, prev_ref[0][7:8, :])
    row = lax.broadcasted_iota(jnp.int32, pa.shape, 0)
    shifted = jnp.where(row == 0, prev_last, pltpu.roll(pa, 1, axis=0))
    x = pa + (shifted - pa) * mu_ref[...]

    r, k, v = x[:, 0:w], x[:, w:2 * w], x[:, 2 * w:3 * w]
    wa, gd = x[:, 3 * w:3 * w + PAIR], x[:, 3 * w + PAIR:3 * w + 2 * PAIR]
    w0, a0, k_k, k_a, r_k, ln_w, ln_b = (vec_ref[i:i + 1, :] for i in range(7))

    lane = lax.broadcasted_iota(jnp.int32, (rows, PAIR), 1)
    ones_blk = ones_ref[...]

    def head_sum(t):
        t_b = t.astype(BF)
        return jnp.concatenate([_dot(t_b[:, p * PAIR:(p + 1) * PAIR], ones_blk) for p in range(n_pairs)], axis=1)

    lora_in = jnp.where(lane < HEAD_DIM, jnp.tanh(wa), wa).astype(BF)
    twa = _dot(lora_in, wwa_ref[...])
    neg = -(w0 + twa[:, :w])
    softplus = jnp.maximum(neg, 0.0) + jnp.log(1.0 + jnp.exp(-jnp.abs(neg)))
    logw = -jnp.exp(-softplus - 0.5)
    a = jax.nn.sigmoid(a0 + twa[:, w:])
    gate = _dot(jax.nn.sigmoid(gd).astype(BF), g2_ref[...])

    kk = k * k_k
    kk = kk * lax.rsqrt(jnp.maximum(head_sum(kk * kk), 1e-24))
    k = k * (1.0 + (a - 1.0) * k_a)
    aa = -kk
    bb = kk * a

    tr = lax.broadcasted_iota(jnp.int32, (rows, rows), 0)
    tc = lax.broadcasted_iota(jnp.int32, (rows, rows), 1)
    tril = jnp.where((tr >= tc) & (tr // c == tc // c), 1.0, 0.0).astype(BF)
    cum = sum(_dot(tril, part) for part in _split_bf16(logw, 2))

    def stack_heads(t):
        lo = lax.broadcasted_iota(jnp.int32, t.shape, 2) < HEAD_DIM
        return jnp.concatenate([jnp.where(lo, t, 0.0), jnp.where(lo, 0.0, t)], axis=1)

    sr = lax.broadcasted_iota(jnp.int32, (2 * c, 2 * c), 0)
    sc = lax.broadcasted_iota(jnp.int32, (2 * c, 2 * c), 1)
    blk_xor = sr ^ sc
    strict, incl = sr % c > sc % c, sr % c >= sc % c

    def lower_rows(t, blk):
        return jnp.concatenate([t[:, r0:r0 + blk] for r0 in range(blk, 2 * c, 2 * blk)], axis=1)

    def scatter_lower(t, blk):
        zero = jnp.zeros((t.shape[0], blk, t.shape[2]), t.dtype)
        parts = []
        for i in range(c // blk):
            parts += [zero, t[:, i * blk:(i + 1) * blk]]
        return jnp.concatenate(parts, axis=1)

    def chunk_prep():
        def to_pairs(t):
            return jnp.stack([t[gi * c:(gi + 1) * c, p * PAIR:(p + 1) * PAIR]
                              for gi in range(g) for p in range(n_pairs)])

        cum_p, logw_p, r_p, k_p, v_p, a_p, b_p = (to_pairs(t) for t in (cum, logw, r, k, v, aa, bb))
        e_in = jnp.exp(cum_p)
        e_out = jnp.exp(-cum_p)
        e_prev = jnp.exp(cum_p - logw_p)
        cum_end = cum_p[:, c - 1:c, :]
        e_rest = jnp.exp(cum_end - cum_p)
        a_st, r_st = stack_heads(a_p * e_prev), stack_heads(r_p * e_in)
        b_rest = stack_heads(b_p * e_rest).astype(BF)
        lhs = jnp.concatenate([a_st, r_st], axis=1).astype(BF)
        rhs = jnp.concatenate([stack_heads(b_p * e_out), stack_heads(k_p * e_out)], axis=1).astype(BF)
        bk_rest = jnp.concatenate([b_rest, stack_heads(k_p * e_rest).astype(BF)], axis=1)
        vs = stack_heads(v_p)
        scores = _bdot_nt(lhs, rhs)
        ab = jnp.where(strict, scores[:, :2 * c, :2 * c], 0.0)
        ak = jnp.where(strict, scores[:, :2 * c, 2 * c:], 0.0).astype(BF)
        rb = jnp.where(incl, scores[:, 2 * c:, :2 * c], 0.0).astype(BF)
        rk = jnp.where(incl, scores[:, 2 * c:, 2 * c:], 0.0).astype(BF)
        akv = _bdot(ak, vs.astype(BF))
        n = jnp.where(blk_xor == 1, ab, 0.0)
        blk = 2
        while blk < c:
            in_corner = (blk_xor & -blk) == blk
            n_b = n.astype(BF)
            if blk < F32_SUBLANES:
                off = jnp.where(in_corner, ab, 0.0)
                x = off + _bdot(off.astype(BF), n_b)
                n = n + x + _bdot(n_b, x.astype(BF))
            else:
                off = jnp.where(lower_rows(in_corner[None], blk), lower_rows(ab, blk), 0.0)
                x = off + _bdot(off.astype(BF), n_b)
                corr = x + _bdot(lower_rows(n, blk).astype(BF), scatter_lower(x, blk).astype(BF))
                n = n + scatter_lower(corr, blk)
            blk *= 2
        rbk = jnp.concatenate([rb, rk], axis=2)
        return lhs, akv, n.astype(BF), rbk, vs, bk_rest, jnp.exp(cum_end)

    def chunk_step(lhs, akv, n_b, rbk, vs, bk_rest, decay_end):
        s0 = s_scr[...]
        ls = _bdot_nt(lhs, s0.astype(BF))
        zz = ls[:, :2 * c] + akv
        u = zz + _bdot(n_b, zz.astype(BF))
        uv = jnp.concatenate([u, vs], axis=1).astype(BF)
        y = ls[:, 2 * c:] + _bdot(rbk, uv)
        s_scr[...] = s0 * decay_end + _bdot_tn(uv, bk_rest)
        y = y[:, :c] + y[:, c:]
        return jnp.concatenate([y[p] for p in range(n_pairs)], axis=1)

    prep = chunk_prep()
    ys = [chunk_step(*(t[gi * n_pairs:(gi + 1) * n_pairs] for t in prep)) for gi in range(g)]
    y = jnp.concatenate(ys, axis=0) if g > 1 else ys[0]

    mean = head_sum(y) * (1.0 / HEAD_DIM)
    dlt = y - mean
    var = head_sum(dlt * dlt) * (1.0 / HEAD_DIM)
    yn = dlt * lax.rsqrt(var + A_GN_EPS) * ln_w + ln_b
    bonus = head_sum(r * k * r_k)
    o_ref[0] = ((yn + bonus * v) * gate).astype(o_ref.dtype)


def rwkv7_mix(pa, mu, wwa, g2, vec):
    b, s, cols = pa.shape
    c = WKV_CHUNK
    rows = min(WKV_GROUP * c, s)
    width = 4 * PAIR
    head = jnp.arange(PAIR, dtype=jnp.int32) // HEAD_DIM
    ones_blk = (head[:, None] == head[None, :]).astype(BF)
    return pl.pallas_call(
        functools.partial(_rwkv_kernel, c=c),
        grid=(b, s // rows),
        in_specs=[pl.BlockSpec((1, rows, cols), lambda i, j: (i, j, 0)),
                  pl.BlockSpec((1, 8, cols), lambda i, j: (i, jnp.maximum(j * (rows // 8) - 1, 0), 0)),
                  _resident((1, cols)), _resident(wwa.shape), _resident(g2.shape), _resident(vec.shape),
                  _resident(ones_blk.shape)],
        out_specs=pl.BlockSpec((1, rows, width), lambda i, j: (i, j, 0)),
        out_shape=jax.ShapeDtypeStruct((b, s, width), BF),
        scratch_shapes=[pltpu.VMEM((4, PAIR, PAIR), F32)],
        compiler_params=_params("arbitrary", "arbitrary"),
        name="rwkv7_mix",
    )(pa, pa, mu.reshape(1, cols), wwa, g2, vec, ones_blk)


def _retention_kernel(pb_ref, rope_ref, tab_ref, dm_ref, lnw_ref, o_ref, s_scr, *, c):
    ci, bi = pl.program_id(0), pl.program_id(1)
    rows = pb_ref.shape[1]
    g = rows // c
    n_pairs = 4
    w = n_pairs * PAIR

    @pl.when(ci == 0)
    def _():
        s_scr[bi] = jnp.zeros(s_scr.shape[1:], F32)

    lane = lax.broadcasted_iota(jnp.int32, (rows, w), 1)
    first_half = (lane % HEAD_DIM) < (HEAD_DIM // 2)
    cos, sin = rope_ref[:, :w], rope_ref[:, w:]

    def rotary(t):
        swapped = jnp.where(first_half, pltpu.roll(t, w - HEAD_DIM // 2, axis=1), pltpu.roll(t, HEAD_DIM // 2, axis=1))
        return t * cos + swapped * sin

    def to_pairs(t):
        return jnp.stack([t[gi * c:(gi + 1) * c, p * PAIR:(p + 1) * PAIR] for gi in range(g) for p in range(n_pairs)])

    def stack_heads(t):
        lo = lax.broadcasted_iota(jnp.int32, t.shape, 2) < HEAD_DIM
        return jnp.concatenate([jnp.where(lo, t, 0.0), jnp.where(lo, 0.0, t)], axis=1)

    q = rotary(pb_ref[0, :, 0:w].astype(F32))
    k = rotary(pb_ref[0, :, w:2 * w].astype(F32)) * (HEAD_DIM ** -0.5)
    gate = pb_ref[0, :, 3 * w:4 * w].astype(F32)
    xi, zeta, cdecay = tab_ref[0], tab_ref[1], tab_ref[2][0:1, :]

    q_p, k_p, qx_p, kz_p = (to_pairs(t) for t in (q, k, q * xi, k * zeta))
    v_p = to_pairs(pb_ref[0, :, 2 * w:3 * w].astype(F32))
    lo = lax.broadcasted_iota(jnp.int32, q_p.shape, 2) < HEAD_DIM
    k_b = k_p.astype(BF)
    dm_lo = jnp.stack([dm_ref[2 * p] for _ in range(g) for p in range(n_pairs)])
    dm_hi = jnp.stack([dm_ref[2 * p + 1] for _ in range(g) for p in range(n_pairs)])
    s_lo = _bdot_nt(jnp.where(lo, q_p, 0.0).astype(BF), k_b) * dm_lo
    s_hi = _bdot_nt(jnp.where(lo, 0.0, q_p).astype(BF), k_b) * dm_hi
    vs = stack_heads(v_p).astype(BF)
    inner = _bdot(jnp.concatenate([s_lo, s_hi], axis=2).astype(BF), vs)
    upd = _bdot_tn(stack_heads(kz_p).astype(BF), vs)

    cd = jnp.stack([cdecay[:, p * PAIR:(p + 1) * PAIR] for p in range(n_pairs)])
    st = s_scr[bi]
    states = []
    for gi in range(g):
        states.append(st)
        st = st * cd + upd[gi * n_pairs:(gi + 1) * n_pairs]
    s_scr[bi] = st
    states = jnp.concatenate(states, axis=0) if g > 1 else states[0]
    o = inner + _bdot(qx_p.astype(BF), states.astype(BF))

    o = jnp.concatenate([jnp.concatenate([o[gi * n_pairs + p] for p in range(n_pairs)], axis=1)
                         for gi in range(g)], axis=0)
    ones_blk = _pair_consts(rows)[1]
    ms = jnp.concatenate([_head_sum((o * o)[:, p * PAIR:(p + 1) * PAIR], ones_blk) for p in range(n_pairs)], axis=1)
    o = o * lax.rsqrt(ms * (1.0 / HEAD_DIM) + NORM_EPS) * lnw_ref[...]
    o_ref[0] = (o * (gate * jax.nn.sigmoid(gate))).astype(o_ref.dtype)


def _retention_tables(s, c, n_heads):
    d = HEAD_DIM
    inv = ROPE_BASE ** (-jnp.arange(0, d, 2, dtype=F32) / d)
    ang = jnp.arange(s, dtype=F32)[:, None] * inv[None, :]
    cos, sin = jnp.cos(ang), jnp.sin(ang)
    cos_full = jnp.tile(jnp.concatenate([cos, cos], axis=1), (1, n_heads))
    sin_signed = jnp.tile(jnp.concatenate([-sin, sin], axis=1), (1, n_heads))
    rope = jnp.concatenate([cos_full, sin_signed], axis=1)
    gamma = 1.0 - 2.0 ** (-5.0 - jnp.arange(n_heads, dtype=F32))
    lg = jnp.log(gamma)[:, None]
    idx = jnp.arange(c, dtype=F32)
    rel = idx[:, None] - idx[None, :]
    dmat = jnp.where(rel >= 0, jnp.exp(jnp.maximum(rel, 0.0)[None] * lg[..., None]), 0.0)
    expand = lambda t: jnp.repeat(t.T, d, axis=1)
    zeta = expand(jnp.exp((c - 1 - idx)[None, :] * lg))
    xi = expand(jnp.exp((idx + 1)[None, :] * lg))
    cdecay = expand(jnp.broadcast_to(jnp.exp(c * lg), (n_heads, c)))
    return rope, jnp.stack([xi, zeta, cdecay]), dmat


def retention_mix(pb, ln_w):
    b, s, cols = pb.shape
    c = RET_CHUNK
    rows = min(RET_GROUP * c, s)
    width = cols // 4
    rope, tab, dmat = _retention_tables(s, c, width // HEAD_DIM)
    tab = jnp.tile(tab, (1, rows // c, 1))
    return pl.pallas_call(
        functools.partial(_retention_kernel, c=c),
        grid=(s // rows, b),
        in_specs=[pl.BlockSpec((1, rows, cols), lambda j, i: (i, j, 0)),
                  pl.BlockSpec((rows, 2 * width), lambda j, i: (j, 0)),
                  _resident(tab.shape), _resident(dmat.shape), _resident((1, width))],
        out_specs=pl.BlockSpec((1, rows, width), lambda j, i: (i, j, 0)),
        out_shape=jax.ShapeDtypeStruct((b, s, width), BF),
        scratch_shapes=[pltpu.VMEM((b, 4, PAIR, PAIR), F32)],
        compiler_params=_params("arbitrary", "arbitrary"),
        name="retention_mix",
    )(pb, rope, tab, dmat, ln_w.reshape(1, width))


def _t5_bucket(rel):
    n = jnp.maximum(rel, 0)
    max_exact = REL_BUCKETS // 2
    large = max_exact + (jnp.log(jnp.maximum(n, max_exact).astype(F32) / max_exact)
                         / math.log(REL_MAX_DIST / max_exact) * (REL_BUCKETS - max_exact)).astype(jnp.int32)
    large = jnp.minimum(large, REL_BUCKETS - 1)
    return jnp.where(n < max_exact, n, large)


def _qkv_proj_kernel(x_ref, g_ref, wqk_ref, wvt_ref, qk_ref, vt_ref, *, n_chunk):
    xn = _rms(x_ref[...], g_ref[...]).astype(BF)
    half = qk_ref.shape[-1] // 2
    for c in range(0, qk_ref.shape[-1], n_chunk):
        scale = ATTN_Q_SCALE if c < half else 1.0
        qk_ref[:, c:c + n_chunk] = (_dot(xn, wqk_ref[:, c:c + n_chunk]) * scale).astype(BF)
    n_heads, n_sub, rows, t = vt_ref.shape[1:]
    vt = _dot_nt(wvt_ref[...], xn)
    for h in range(n_heads):
        for j in range(n_sub):
            vt_ref[0, h, j, :PAIR, :] = vt[h * PAIR:(h + 1) * PAIR, j * t:(j + 1) * t].astype(BF)
            vt_ref[0, h, j, PAIR:, :] = jnp.ones((rows - PAIR, t), BF)


def qkv_proj(x, g, w_qk, w_vt, b, tm=1024, n_chunk=512):
    m, k = x.shape
    s = m // b
    tm = min(tm, s)
    t = min(ATTN_TILE, s)
    n_heads = w_vt.shape[0] // PAIR
    per_b = s // tm
    return pl.pallas_call(
        functools.partial(_qkv_proj_kernel, n_chunk=n_chunk),
        grid=(m // tm,),
        in_specs=[pl.BlockSpec((tm, k), lambda i: (i, 0)), _resident((1, k)), w_qk.spec(), w_vt.spec()],
        out_specs=[pl.BlockSpec((tm, w_qk.shape[1]), lambda i: (i, 0)),
                   pl.BlockSpec((1, n_heads, tm // t, VT_ROWS, t), lambda i: (i // per_b, 0, i % per_b, 0, 0))],
        out_shape=[jax.ShapeDtypeStruct((m, w_qk.shape[1]), BF),
                   jax.ShapeDtypeStruct((b, n_heads, s // t, VT_ROWS, t), BF)],
        compiler_params=_params("parallel"),
        name="qkv_proj",
    )(x, g.reshape(1, k), w_qk.array, w_vt.array)


def _attn_schedule(n_q):
    near = [(qi, qi, 0) for qi in range(n_q)] + [(qi - 1, qi, 1) for qi in range(1, n_q)]
    far = [(kj, qi, 2) for qi in range(n_q) for kj in range(qi - 1)]
    entries = [(0, 0, 2)] + near + far
    n_biased = len(near) - 1
    return entries, len(near) + len(far), n_biased + n_biased % 2


def _diff_attn_kernel(sched_ref, relb_ref, lqk_ref, q_ref, k_ref, vt_ref, bkt_ref, lnw_ref, o_ref,
                      bias_scr, m_scr, acc_scr, s_scr, *, lam_init, entries, n_entries, n_bias_steps):
    hi, bi = pl.program_id(0), pl.program_id(1)
    t = bias_scr.shape[1]
    n_q = q_ref.shape[1] // t
    far_bias = relb_ref[REL_BUCKETS - 1, hi]

    @pl.when(bi == 0)
    def _():
        for tile in range(2):
            bkt = bkt_ref[tile]
            bias = jnp.full((t, t), MASK_VALUE, F32)
            for bucket in range(REL_BUCKETS):
                bias = jnp.where(bkt == bucket, (relb_ref[bucket, hi] - far_bias) * LOG2_E, bias)
            bias_scr[tile] = bias
        bias_scr[2] = jnp.zeros((t, t), F32)

    k_bias = max(d for d in range(2, ATTN_UNROLL_BIASED + 1, 2) if n_bias_steps % d == 0) if n_bias_steps else 2
    static_first = n_bias_steps // k_bias <= 1
    if not static_first:
        m_scr[...] = jnp.full(m_scr.shape, MASK_VALUE, F32)
        acc_scr[...] = jnp.zeros(acc_scr.shape, F32)
    lane = lax.broadcasted_iota(jnp.int32, (t, PAIR), 1)

    def entry(i):
        if isinstance(i, int):
            return entries[i]
        return sched_ref[0, i], sched_ref[1, i], sched_ref[2, i]

    def tile(ref, j):
        return ref[0, j * t:(j + 1) * t, :] if isinstance(j, int) else ref[0, pl.ds(pl.multiple_of(j * t, t), t), :]

    def step(i, slot, with_bias, score=True, absorb=True):
        if score:
            kj, qt, bidx = entry(i + 1)
            k_t, q = tile(k_ref, kj), tile(q_ref, qt)
        if absorb:
            kj_i, st, bidx_i = entry(i)
            v_t = vt_ref[0, 0, kj_i]
            first = static_first and isinstance(i, int) and bidx_i == 0
        for c in range(2):
            if score:
                q_c = jnp.where(lane < HEAD_DIM, q, 0.0) if c == 0 else jnp.where(lane < HEAD_DIM, 0.0, q)
                s = _dot_nt(k_t, q_c.astype(BF))
                s_scr[slot, c] = s + bias_scr[bidx] if with_bias else s
            if absorb:
                s = s_scr[1 - slot, c]
                m_new = jnp.max(s, axis=0, keepdims=True)
                if not first:
                    m_old = m_scr[st, c]
                    m_new = jnp.maximum(m_old, m_new)
                pv = _dot(v_t, jnp.exp2(s - m_new).astype(BF))
                acc_scr[st, c] = pv if first else jnp.exp2(m_old - m_new) * acc_scr[st, c] + pv
                m_scr[st, c] = m_new

    def run(n_iter, first, k, with_bias):
        def body(j, carry):
            for r in range(k):
                step(first + k * j + r, (first + r) % 2, with_bias)
            return carry
        if n_iter == 1:
            body(0, 0)
        elif n_iter > 1:
            lax.fori_loop(0, n_iter, body, 0)

    step(0, 0, True, absorb=False)
    run(n_bias_steps // k_bias, 1, k_bias, True)
    first_far, n_far_steps = n_bias_steps + 1, n_entries - 1 - n_bias_steps
    run(n_far_steps // ATTN_UNROLL, first_far, ATTN_UNROLL, False)
    for i in range(n_entries - n_far_steps % ATTN_UNROLL, n_entries):
        step(i, i % 2, False)
    step(n_entries, n_entries % 2, False, score=False)

    lam = (jnp.exp(jnp.sum(lqk_ref[0:1, :] * lqk_ref[1:2, :], axis=-1, keepdims=True))
           - jnp.exp(jnp.sum(lqk_ref[2:3, :] * lqk_ref[3:4, :], axis=-1, keepdims=True)) + lam_init)
    for qi in range(n_q):
        acc1, acc2 = acc_scr[qi, 0], acc_scr[qi, 1]
        o_t = (acc1[:PAIR] * (1.0 / acc1[PAIR:PAIR + 1])
               - lam * (acc2[:PAIR] * (1.0 / acc2[PAIR:PAIR + 1])))
        o_t = o_t * lax.rsqrt(jnp.mean(o_t * o_t, axis=0, keepdims=True) + NORM_EPS)
        o_ref[0, qi * t:(qi + 1) * t, :] = (o_t.T * lnw_ref[...] * (1.0 - lam_init)).astype(o_ref.dtype)


def diff_attention(qk, vt, rel_bias, lqk, ln_w, lam_init):
    b, s, cols = qk.shape
    d = cols // 2
    n_heads = d // PAIR
    t = vt.shape[-1]
    n_q = s // t
    pos = jnp.arange(t, dtype=jnp.int32)
    rel = pos[None, :] - pos[:, None]
    buckets = jnp.stack([jnp.where(rel >= 0, _t5_bucket(rel), -1), _t5_bucket(rel + t)])
    entries, n_entries, n_bias_steps = _attn_schedule(n_q)
    sched = jnp.asarray(entries, dtype=jnp.int32).T
    return pl.pallas_call(
        functools.partial(_diff_attn_kernel, lam_init=lam_init, entries=tuple(entries), n_entries=n_entries,
                          n_bias_steps=n_bias_steps),
        grid=(n_heads, b),
        in_specs=[pl.BlockSpec(memory_space=pltpu.SMEM),
                  pl.BlockSpec(memory_space=pltpu.SMEM),
                  _resident(lqk.shape),
                  pl.BlockSpec((1, s, PAIR), lambda h, i: (i, 0, h)),
                  pl.BlockSpec((1, s, PAIR), lambda h, i: (i, 0, n_heads + h)),
                  pl.BlockSpec((1, 1, n_q, VT_ROWS, t), lambda h, i: (i, h, 0, 0, 0)),
                  _resident(buckets.shape), _resident((1, PAIR))],
        out_specs=pl.BlockSpec((1, s, PAIR), lambda h, i: (i, 0, h)),
        out_shape=jax.ShapeDtypeStruct((b, s, d), BF),
        scratch_shapes=[pltpu.VMEM((3, t, t), F32),
                        pltpu.VMEM((n_q, 2, 1, t), F32),
                        pltpu.VMEM((n_q, 2, VT_ROWS, t), F32),
                        pltpu.VMEM((2, 2, t, t), F32)],
        compiler_params=_params("arbitrary", "arbitrary"),
        name="diff_attention",
    )(sched, rel_bias, lqk, qk, qk, vt, buckets, ln_w.reshape(1, PAIR))


def kernel(x, mem, rel_bias, mem_norm_w, final_norm_w, norm_mix_w, norm_cross_w, norm_mlp_w, xattn_w_q, xattn_w_kv, xattn_w_o, mlp_w1, mlp_w2, hyb_w_in, rwkv_mu, rwkv_w0, rwkv_w2, rwkv_a0, rwkv_a2, rwkv_g2, rwkv_k_k, rwkv_k_a, rwkv_r_k, rwkv_ln_w, rwkv_ln_b, ret_ln_w, hyb_w_out, diff_w_in, diff_lq1, diff_lk1, diff_lq2, diff_lk2, diff_ln_w, diff_w_out):
    b, s, d = x.shape
    n_mem = mem.shape[1]
    depth = norm_mix_w.shape[0]
    a_width = rwkv_w0.shape[1]
    a_cols = rwkv_mu.shape[1]
    m = b * s
    bf = lambda t: t.astype(BF)
    hyb_w_in, hyb_w_out, diff_w_out = bf(hyb_w_in), bf(hyb_w_out), bf(diff_w_out)
    diff_w_qk = bf(diff_w_in[:, :, :2 * d])
    xattn_w_q, xattn_w_kv, xattn_w_o = bf(xattn_w_q), bf(xattn_w_kv), bf(xattn_w_o)
    mlp_w1, mlp_w2 = bf(mlp_w1), bf(mlp_w2)

    h = x.reshape(m, d)
    mem2 = mem.reshape(b * n_mem, d)
    for layer in range(depth):
        i = layer // 2
        if layer % 2 == 0:
            pa, pb = norm_matmul(h, norm_mix_w[layer], _of_layer(hyb_w_in, i),
                                 (a_cols, hyb_w_in.shape[2] - a_cols), (F32, BF))
            lora = rwkv_w2.shape[1]
            zeros = jnp.zeros((lora, a_width), F32)
            wwa = bf(jnp.concatenate([jnp.concatenate([rwkv_w2[i], zeros], axis=1),
                                      jnp.concatenate([zeros, rwkv_a2[i]], axis=1)], axis=0))
            vec = jnp.stack([rwkv_w0[i], rwkv_a0[i], rwkv_k_k[i], rwkv_k_a[i], rwkv_r_k[i].reshape(-1),
                             rwkv_ln_w[i], rwkv_ln_b[i], jnp.zeros((a_width,), F32)])
            y_a = rwkv7_mix(pa.reshape(b, s, -1), rwkv_mu[i], wwa, bf(rwkv_g2[i]), vec)
            y_b = retention_mix(pb.reshape(b, s, -1), ret_ln_w[i])
            ys = [y_a, y_b]
            ws = [_of_layer(hyb_w_out, i, rows=a_width, row_block=0), _of_layer(hyb_w_out, i, rows=a_width, row_block=1)]
        else:
            lam_init = 0.8 - 0.6 * math.exp(-0.3 * layer)
            qk, vt = qkv_proj(h, norm_mix_w[layer], _of_layer(diff_w_qk, i),
                              _whole(bf(diff_w_in[i][:, 2 * d:].T)), b)
            lqk = jnp.stack([diff_lq1[i], diff_lk1[i], diff_lq2[i], diff_lk2[i]])
            o = diff_attention(qk.reshape(b, s, -1), vt, rel_bias, lqk, diff_ln_w[i], lam_init)
            ys, ws = [o], [_of_layer(diff_w_out, i)]
        (kv,) = norm_matmul(mem2, mem_norm_w, _of_layer(xattn_w_kv, layer), (2 * d,), (BF,))
        h = xattn_block(ys, ws, h.reshape(b, s, d), norm_cross_w[layer], _of_layer(xattn_w_q, layer),
                        kv.reshape(b, n_mem, 2 * d), _of_layer(xattn_w_o, layer)).reshape(m, d)
        h = mlp_block(h, norm_mlp_w[layer], _of_layer(mlp_w1, layer), _of_layer(mlp_w2, layer), final_norm_w,
                      final_norm=(layer == depth - 1))
    return h.reshape(b, s, d)
```
